```python
import math
import jax, jax.numpy as jnp
from jax import lax
import numpy as np


D_MODEL = 1024
BATCH = 2
SEQ = 8192
DEPTH = 2

CHUNK = 64
N_EVEN = (DEPTH + 1) // 2
N_ODD = DEPTH // 2
CONV_W = 4
NORM_EPS = 1e-6
F32 = jnp.float32

W_MIX = D_MODEL
W_A = D_MODEL // 2
NH_A = 8
HW_A = W_A // NH_A
RG_C = 8.0
NH_B = 4
DK_B = 64
DV_B = 128
K_B = NH_B * DK_B
V_B = NH_B * DV_B
R_GATE = 16
GATE_NORM = 16.0
P_AB = 2 * W_A + 2 * K_B + 2 * V_B + R_GATE

NH_C = 8
DH_C = 128
W_C = NH_C * DH_C
P_C = 4 * W_C + 2 * NH_C

N_GROUPS = 4
EXP_PER_GROUP = 8
N_EXPERTS = N_GROUPS * EXP_PER_GROUP
TOP_K = 2
D_EXPERT = 512
ROW_BLOCK = 128

kernel_name = 'hybrid_rglru_gla_gdn_hmoe_adaln'


def _offsets(sizes):
    out, acc = [], 0
    for s in sizes[:-1]:
        acc += s
        out.append(acc)
    return out


def rms_norm(x, g):
    x32 = x.astype(F32)
    y = x32 * lax.rsqrt(jnp.mean(x32 * x32, axis=-1, keepdims=True) + NORM_EPS)
    return (y * g.astype(F32)).astype(x.dtype)


def head_rms_norm(o, g):
    return o * lax.rsqrt(jnp.mean(o * o, axis=-1, keepdims=True) + NORM_EPS) * g.astype(F32)


def l2_norm(x):
    return x * lax.rsqrt(jnp.sum(x * x, axis=-1, keepdims=True) + NORM_EPS)


def causal_conv(x, w):
    return lax.conv_general_dilated(
        x, w[:, None, :].astype(x.dtype), window_strides=(1,), padding=[(CONV_W - 1, 0)],
        dimension_numbers=('NWC', 'WIO', 'NWC'), feature_group_count=x.shape[-1])


def to_chunks(t):
    B, S, H = t.shape[:3]
    t = t.reshape((B, S // CHUNK, CHUNK, H) + t.shape[3:])
    return jnp.moveaxis(t, (1, 3), (0, 2))


def from_chunks(t):
    t = jnp.moveaxis(t, (0, 2), (1, 3))
    B, N, C, H = t.shape[:4]
    return t.reshape((B, N * C, H) + t.shape[4:])


def _linear_recurrence_combine(left, right):
    a_l, b_l = left
    a_r, b_r = right
    return a_l * a_r, a_r * b_l + b_r


def rg_lru(x, wa, ba, wx, bx, lam):
    B, S, _ = x.shape
    xh = x.reshape(B, S, NH_A, HW_A)
    r = jax.nn.sigmoid((jnp.einsum('bshi,hij->bshj', xh, wa).reshape(B, S, W_A) + ba).astype(F32))
    i = jax.nn.sigmoid((jnp.einsum('bshi,hij->bshj', xh, wx).reshape(B, S, W_A) + bx).astype(F32))
    log_a = -RG_C * r * jax.nn.softplus(-lam.astype(F32))
    a = jnp.exp(log_a)
    b = jnp.sqrt(-jnp.expm1(2.0 * log_a)) * (i * x.astype(F32))
    _, h = lax.associative_scan(_linear_recurrence_combine, (a, b), axis=1)
    return h


def gla_chunked(q, k, v, lg):
    B, H = q.shape[0], q.shape[2]
    qc, kc, vc = to_chunks(q), to_chunks(k), to_chunks(v)
    gcum = jnp.cumsum(to_chunks(lg), axis=3)
    causal = jnp.tril(jnp.ones((CHUNK, CHUNK), dtype=bool))

    def step(state, inp):
        qb, kb, vb, gb = inp
        g_last = gb[:, :, -1:, :]
        o_inter = jnp.einsum('bhcd,bhde->bhce', qb * jnp.exp(gb), state)
        diff = jnp.where(causal[:, :, None], gb[:, :, :, None, :] - gb[:, :, None, :, :], -jnp.inf)
        att = jnp.einsum('bhid,bhjd,bhijd->bhij', qb, kb, jnp.exp(diff))
        o = o_inter + jnp.einsum('bhij,bhje->bhie', att, vb)
        state = state * jnp.exp(g_last[:, :, 0, :, None]) + jnp.einsum(
            'bhcd,bhce->bhde', kb * jnp.exp(g_last - gb), vb)
        return state, o

    s0 = jnp.zeros((B, H, DK_B, DV_B), F32)
    _, o = lax.scan(step, s0, (qc, kc, vc, gcum))
    return from_chunks(o)


def gated_delta_chunked(q, k, v, beta, g):
    B, H, DV = q.shape[0], q.shape[2], v.shape[-1]
    qc, kc, vc = to_chunks(q), to_chunks(k), to_chunks(v)
    bc = to_chunks(beta)
    gcum = jnp.cumsum(to_chunks(g), axis=-1)
    idx = jnp.arange(CHUNK)
    incl = idx[:, None] >= idx[None, :]
    strict = idx[:, None] > idx[None, :]
    decay = jnp.exp(jnp.where(incl, gcum[..., :, None] - gcum[..., None, :], -jnp.inf))
    k_beta = kc * bc[..., None]
    lower = jnp.where(strict, jnp.einsum('...id,...jd->...ij', k_beta, kc) * decay, 0.0)
    rhs = jnp.concatenate([vc * bc[..., None], k_beta * jnp.exp(gcum)[..., None]], axis=-1)
    sol = lax.linalg.triangular_solve(lower, rhs, left_side=True, lower=True, unit_diagonal=True)
    u_c, w_c = sol[..., :DV], sol[..., DV:]

    def step(state, inp):
        qb, kb, gb, ub, wb, db = inp
        v_new = ub - jnp.einsum('bhck,bhkv->bhcv', wb, state)
        att = jnp.einsum('bhik,bhjk->bhij', qb, kb) * db
        o = jnp.einsum('bhck,bhkv->bhcv', qb * jnp.exp(gb)[..., None], state) + jnp.einsum(
            'bhij,bhjv->bhiv', att, v_new)
        g_last = gb[..., -1:]
        state = state * jnp.exp(g_last)[..., None] + jnp.einsum(
            'bhck,bhcv->bhkv', kb * jnp.exp(g_last - gb)[..., None], v_new)
        return state, o

    s0 = jnp.zeros((B, H, q.shape[-1], DV), F32)
    _, o = lax.scan(step, s0, (qc, kc, gcum, u_c, w_c, decay))
    return from_chunks(o)


def mixer_ab(u, w_in, conv_w, conv_b, rg_wa, rg_ba, rg_wx, rg_bx, rg_lam, gla_wg2, gla_bg2, gla_norm, w_out):
    B, S, _ = u.shape
    xa, ga, q, k, v, og, gl = jnp.split(
        u @ w_in, _offsets([W_A, W_A, K_B, K_B, V_B, V_B, R_GATE]), axis=-1)
    xa = causal_conv(xa, conv_w) + conv_b
    ya = jax.nn.gelu(ga.astype(F32)) * rg_lru(xa, rg_wa, rg_ba, rg_wx, rg_bx, rg_lam)
    q = q.astype(F32).reshape(B, S, NH_B, DK_B) * DK_B ** -0.5
    k = k.astype(F32).reshape(B, S, NH_B, DK_B)
    v = v.astype(F32).reshape(B, S, NH_B, DV_B)
    lg = jax.nn.log_sigmoid((gl @ gla_wg2 + gla_bg2).astype(F32)).reshape(B, S, NH_B, DK_B) / GATE_NORM
    ob = head_rms_norm(gla_chunked(q, k, v, lg), gla_norm) * jax.nn.silu(og.astype(F32)).reshape(B, S, NH_B, DV_B)
    y = jnp.concatenate([ya, ob.reshape(B, S, V_B)], axis=-1).astype(u.dtype)
    return y @ w_out


def mixer_c(u, w_in, conv_w, a_log, dt_bias, norm_g, w_out):
    B, S, _ = u.shape
    qkv, z, b_logit, a_logit = jnp.split(u @ w_in, _offsets([3 * W_C, W_C, NH_C, NH_C]), axis=-1)
    qkv = jax.nn.silu(causal_conv(qkv, conv_w).astype(F32))
    q, k, v = jnp.split(qkv, 3, axis=-1)
    q = l2_norm(q.reshape(B, S, NH_C, DH_C)) * DH_C ** -0.5
    k = l2_norm(k.reshape(B, S, NH_C, DH_C))
    v = v.reshape(B, S, NH_C, DH_C)
    beta = jax.nn.sigmoid(b_logit.astype(F32))
    g = -jnp.exp(a_log.astype(F32)) * jax.nn.softplus(a_logit.astype(F32) + dt_bias.astype(F32))
    o = gated_delta_chunked(q, k, v, beta, g)
    o = head_rms_norm(o, norm_g) * jax.nn.silu(z.astype(F32)).reshape(B, S, NH_C, DH_C)
    return o.reshape(B, S, W_C).astype(u.dtype) @ w_out


def hier_moe(u, w_grp, b_grp, w_rt, b_rt, w1, w3, w2):
    B, S, D = u.shape
    T = B * S
    TK = T * TOP_K
    xt = u.reshape(T, D)
    pg = jax.nn.softmax((xt @ w_grp + b_grp).astype(F32), axis=-1)
    pg_top, g_idx = lax.top_k(pg, 1)
    le = (xt @ w_rt + b_rt).astype(F32).reshape(T, N_GROUPS, EXP_PER_GROUP)
    le = jnp.einsum('tge,tg->te', le, jax.nn.one_hot(g_idx[:, 0], N_GROUPS, dtype=F32))
    pe_top, e_idx = lax.top_k(jax.nn.softmax(le, axis=-1), TOP_K)
    w_tok = pg_top * pe_top / jnp.sum(pe_top, axis=-1, keepdims=True)
    eid = (g_idx * EXP_PER_GROUP + e_idx).reshape(TK)
    tok = jnp.repeat(jnp.arange(T, dtype=jnp.int32), TOP_K)
    wts = w_tok.reshape(TK)
    order = jnp.argsort(eid)
    se = eid[order]
    counts = jnp.bincount(eid, length=N_EXPERTS)
    padded = (counts + ROW_BLOCK - 1) // ROW_BLOCK * ROW_BLOCK
    start_s = jnp.cumsum(counts) - counts
    end_p = jnp.cumsum(padded)
    start_p = end_p - padded
    dest = start_p[se] + jnp.arange(TK) - start_s[se]
    n_blocks = (TK + ROW_BLOCK - 1) // ROW_BLOCK + N_EXPERTS
    n_rows = n_blocks * ROW_BLOCK
    buf_tok = jnp.full((n_rows,), T, jnp.int32).at[dest].set(tok[order])
    buf_w = jnp.zeros((n_rows,), F32).at[dest].set(wts[order])
    blk_e = jnp.minimum(
        jnp.sum(jnp.arange(n_blocks)[:, None] * ROW_BLOCK >= end_p[None, :], axis=1), N_EXPERTS - 1)
    xpad = jnp.concatenate([xt, jnp.zeros((1, D), xt.dtype)], axis=0)
    xb = xpad[buf_tok].reshape(n_blocks, ROW_BLOCK, D)

    def expert_block(args):
        xr, e = args
        return (jax.nn.silu(xr @ w1[e]) * (xr @ w3[e])) @ w2[e]

    yb = lax.map(expert_block, (xb, blk_e))
    y = jnp.zeros((T + 1, D), F32).at[buf_tok].add(yb.reshape(n_rows, D).astype(F32) * buf_w[:, None])
    return y[:T].reshape(B, S, D).astype(u.dtype)


def setup_inputs(seed: int = 0) -> dict:
    key = jax.random.key(seed)
    k = jax.random.split(key, 40)
    D = D_MODEL

    def nrm(i, shape, scale):
        return jax.random.normal(k[i], shape, jnp.float32) * scale

    def unif(i, shape, lo, hi):
        return jax.random.uniform(k[i], shape, jnp.float32, lo, hi)

    a_rg = unif(13, (N_EVEN, W_A), 0.9, 0.999) ** (1.0 / RG_C)
    dt = jnp.exp(unif(21, (N_ODD, NH_C), math.log(1e-3), math.log(1e-1)))
    return {
        'x': nrm(0, (BATCH, SEQ, D), 1.0),
        'c': nrm(1, (BATCH, D), 1.0),
        'norm1': 1.0 + nrm(2, (DEPTH, D), 0.02),
        'norm2': 1.0 + nrm(3, (DEPTH, D), 0.02),
        'w_ada': nrm(4, (DEPTH, D, 6 * D), 0.5 * D ** -0.5),
        'b_ada': nrm(5, (DEPTH, 6 * D), 0.02),
        'w_in_ab': nrm(6, (N_EVEN, D, P_AB), D ** -0.5),
        'conv_a_w': nrm(7, (N_EVEN, CONV_W, W_A), CONV_W ** -0.5),
        'conv_a_b': nrm(8, (N_EVEN, W_A), 0.02),
        'rg_wa': nrm(9, (N_EVEN, NH_A, HW_A, HW_A), HW_A ** -0.5),
        'rg_ba': nrm(10, (N_EVEN, W_A), 0.02),
        'rg_wx': nrm(11, (N_EVEN, NH_A, HW_A, HW_A), HW_A ** -0.5),
        'rg_bx': nrm(12, (N_EVEN, W_A), 0.02),
        'rg_lam': jnp.log(a_rg) - jnp.log1p(-a_rg),
        'gla_wg2': nrm(14, (N_EVEN, R_GATE, K_B), R_GATE ** -0.5),
        'gla_bg2': nrm(15, (N_EVEN, K_B), 0.02),
        'gla_norm': 1.0 + nrm(16, (N_EVEN, DV_B), 0.02),
        'w_out_ab': nrm(17, (N_EVEN, W_MIX, D), W_MIX ** -0.5),
        'w_in_c': nrm(18, (N_ODD, D, P_C), D ** -0.5),
        'conv_c_w': nrm(19, (N_ODD, CONV_W, 3 * W_C), CONV_W ** -0.5),
        'dn_a_log': jnp.log(unif(20, (N_ODD, NH_C), 1.0, 16.0)),
        'dn_dt_bias': dt + jnp.log(-jnp.expm1(-dt)),
        'dn_norm': 1.0 + nrm(22, (N_ODD, DH_C), 0.02),
        'w_out_c': nrm(23, (N_ODD, W_C, D), W_C ** -0.5),
        'moe_w_grp': nrm(24, (DEPTH, D, N_GROUPS), D ** -0.5),
        'moe_b_grp': nrm(25, (DEPTH, N_GROUPS), 0.01),
        'moe_w_rt': nrm(26, (DEPTH, D, N_EXPERTS), D ** -0.5),
        'moe_b_rt': nrm(27, (DEPTH, N_EXPERTS), 0.01),
        'moe_w1': nrm(28, (DEPTH, N_EXPERTS, D, D_EXPERT), D ** -0.5),
        'moe_w3': nrm(29, (DEPTH, N_EXPERTS, D, D_EXPERT), D ** -0.5),
        'moe_w2': nrm(30, (DEPTH, N_EXPERTS, D_EXPERT, D), D_EXPERT ** -0.5),
        'final_norm': 1.0 + nrm(31, (D,), 0.02),
    }


def reference(x, c, norm1, norm2, w_ada, b_ada, w_in_ab, conv_a_w, conv_a_b, rg_wa, rg_ba, rg_wx, rg_bx,
              rg_lam, gla_wg2, gla_bg2, gla_norm, w_out_ab, w_in_c, conv_c_w, dn_a_log, dn_dt_bias, dn_norm,
              w_out_c, moe_w_grp, moe_b_grp, moe_w_rt, moe_b_rt, moe_w1, moe_w3, moe_w2, final_norm):
    cond = jax.nn.silu(c)
    h = x
    for layer in range(DEPTH):
        mod = cond @ w_ada[layer] + b_ada[layer]
        sh1, sc1, gt1, sh2, sc2, gt2 = jnp.split(mod[:, None, :], 6, axis=-1)
        u = rms_norm(h, norm1[layer]) * (1.0 + sc1) + sh1
        j = layer // 2
        if layer % 2 == 0:
            m = mixer_ab(u, w_in_ab[j], conv_a_w[j], conv_a_b[j], rg_wa[j], rg_ba[j], rg_wx[j], rg_bx[j],
                         rg_lam[j], gla_wg2[j], gla_bg2[j], gla_norm[j], w_out_ab[j])
        else:
            m = mixer_c(u, w_in_c[j], conv_c_w[j], dn_a_log[j], dn_dt_bias[j], dn_norm[j], w_out_c[j])
        h = h + gt1 * m
        u = rms_norm(h, norm2[layer]) * (1.0 + sc2) + sh2
        h = h + gt2 * hier_moe(u, moe_w_grp[layer], moe_b_grp[layer], moe_w_rt[layer], moe_b_rt[layer],
                               moe_w1[layer], moe_w3[layer], moe_w2[layer])
    return rms_norm(h, final_norm)
```

```python
import functools

import jax
import jax.numpy as jnp
from jax import lax
from jax.experimental import pallas as pl
from jax.experimental.pallas import tpu as pltpu

F32 = jnp.float32
BF16 = jnp.bfloat16
I32 = jnp.int32
HI = lax.Precision.HIGHEST

D = 1024
CHUNK = 64
SUB = 16
CONV_W = 4
EPS = 1e-6
LANES = 128
SUBLANES = 8

W_A = D // 2
NH_A = 8
HW_A = W_A // NH_A
RG_C = 8.0
NH_B = 4
DK_B = 64
DV_B = 128
K_B = NH_B * DK_B
V_B = NH_B * DV_B
DKP_B = LANES
KP_B = NH_B * DKP_B
R_GATE = 16
GATE_NORM = 16.0

NH_C = 8
DH_C = 128
W_C = NH_C * DH_C

N_GROUPS = 4
EPG = 8
N_EXP = N_GROUPS * EPG
TOP_K = 2
D_EXP = 512

VMEM_LIMIT = 56 * 1024 * 1024

TM_PROJ = 512
TT_SEQ = 256
BM_MOE = 256
TD_MOE = 256


def _cparams(sem):
    return pltpu.CompilerParams(dimension_semantics=sem, vmem_limit_bytes=VMEM_LIMIT)


def _softplus(x):
    return jnp.maximum(x, 0.0) + jnp.log1p(jnp.exp(-jnp.abs(x)))


def _sigmoid(x):
    return jax.nn.sigmoid(x)


def _silu(x):
    return x * jax.nn.sigmoid(x)


def _rms_mod(h, g, sc, sh):
    y = h * lax.rsqrt(jnp.mean(h * h, axis=-1, keepdims=True) + EPS)
    return y * g * (1.0 + sc) + sh


def _dot(a, b):
    return jnp.dot(a.astype(BF16), b.astype(BF16), preferred_element_type=F32)


def _dot_nt(a, b):
    return lax.dot_general(a.astype(BF16), b.astype(BF16), (((1,), (1,)), ((), ())),
                           preferred_element_type=F32)


def _dot_tn(a, b):
    return lax.dot_general(a.astype(BF16), b.astype(BF16), (((0,), (0,)), ((), ())),
                           preferred_element_type=F32)


def _dot_hi(a, b):
    return jnp.dot(a, b, precision=HI, preferred_element_type=F32)


def _ada_body(c_ref, w_ref, b_ref, o_ref):
    o_ref[...] = _dot_hi(_silu(c_ref[...]), w_ref[...]) + b_ref[...]


def _ada_mod(c, w_ada, b_ada):
    depth, _, n6 = w_ada.shape
    bsz = c.shape[0]
    nt = 1536
    c8 = jnp.zeros((SUBLANES, D), F32).at[:bsz].set(c)
    out = pl.pallas_call(
        _ada_body,
        out_shape=jax.ShapeDtypeStruct((depth, SUBLANES, n6), F32),
        grid=(depth, n6 // nt),
        in_specs=[
            pl.BlockSpec((SUBLANES, D), lambda l, j: (0, 0)),
            pl.BlockSpec((None, D, nt), lambda l, j: (l, 0, j)),
            pl.BlockSpec((None, 1, nt), lambda l, j: (l, 0, j)),
        ],
        out_specs=pl.BlockSpec((None, SUBLANES, nt), lambda l, j: (l, 0, j)),
        compiler_params=_cparams(("arbitrary", "arbitrary")),
        name="ada_mod",
    )(c8, w_ada, b_ada.reshape(depth, 1, n6))
    return out[:, :bsz].reshape(depth, bsz, 6, D)


def _inproj_body(widths, h_ref, mod_ref, g_ref, w_ref, *out_refs):
    u = _rms_mod(h_ref[...], g_ref[...], mod_ref[1:2, :], mod_ref[0:1, :]).astype(BF16)
    off = 0
    for o_ref, wd in zip(out_refs, widths):
        o_ref[...] = jnp.dot(u, w_ref[:, off:off + wd], preferred_element_type=F32)
        off += wd


def _inproj(h, mod, g, w_bf16, widths, seq):
    t_rows = h.shape[0]
    tm = min(TM_PROJ, seq)
    per_b = seq // tm
    p = w_bf16.shape[1]
    assert sum(widths) == p
    return pl.pallas_call(
        functools.partial(_inproj_body, widths),
        out_shape=[jax.ShapeDtypeStruct((t_rows, wd), F32) for wd in widths],
        grid=(t_rows // tm,),
        in_specs=[
            pl.BlockSpec((tm, D), lambda i: (i, 0)),
            pl.BlockSpec((None, 6, D), lambda i: (i // per_b, 0, 0)),
            pl.BlockSpec((1, D), lambda i: (0, 0)),
            pl.BlockSpec((D, p), lambda i: (0, 0)),
        ],
        out_specs=[pl.BlockSpec((tm, wd), lambda i: (i, 0)) for wd in widths],
        compiler_params=_cparams(("arbitrary",)),
        name="inproj",
    )(h, mod, g.reshape(1, D), w_bf16)


def _causal_conv(x_ref, cw_ref, xbuf, first):
    tt = x_ref.shape[0]

    @pl.when(first)
    def _():
        xbuf[0:SUBLANES, :] = jnp.zeros((SUBLANES, xbuf.shape[1]), F32)

    xbuf[SUBLANES:SUBLANES + tt, :] = x_ref[...]
    acc = None
    for w in range(CONV_W):
        lo = SUBLANES - (CONV_W - 1) + w
        term = cw_ref[w:w + 1, :] * xbuf[lo:lo + tt, :]
        acc = term if acc is None else acc + term
    xbuf[0:SUBLANES, :] = xbuf[tt:tt + SUBLANES, :]
    return acc


def _rglru_body(xa_ref, ga_ref, cw_ref, cb_ref, wg_ref, bg_ref, lam_ref, o_ref, xbuf, hc):
    t = pl.program_id(1)
    tt = xa_ref.shape[0]

    @pl.when(t == 0)
    def _():
        hc[...] = jnp.zeros_like(hc)

    xc = _causal_conv(xa_ref, cw_ref, xbuf, t == 0) + cb_ref[...]
    gates = _dot(xc, wg_ref[...]) + bg_ref[...]
    r = _sigmoid(gates[:, :W_A])
    i = _sigmoid(gates[:, W_A:])
    log_a = -RG_C * r * _softplus(-lam_ref[...])
    a = jnp.exp(log_a)
    b = jnp.sqrt(1.0 - a * a) * (i * xc)
    row = lax.broadcasted_iota(I32, (tt, W_A), 0)
    s = 1
    while s < tt:
        a_sh = pltpu.roll(a, s, 0)
        b_sh = pltpu.roll(b, s, 0)
        valid = row >= s
        b = jnp.where(valid, a * b_sh + b, b)
        a = jnp.where(valid, a * a_sh, a)
        s *= 2
    h = b + a * hc[0:1, :]
    hc[...] = jnp.broadcast_to(h[tt - 1:tt, :], hc.shape)
    o_ref[...] = jax.nn.gelu(ga_ref[...], approximate=True) * h


def _rglru(xa, ga, conv_w, conv_b, wg_bd, bg, lam, bsz, seq):
    tt = min(TT_SEQ, seq)
    per_b = seq // tt
    row = lambda b, t: (b * per_b + t, 0)
    const = lambda b, t: (0, 0)
    return pl.pallas_call(
        _rglru_body,
        out_shape=jax.ShapeDtypeStruct(xa.shape, F32),
        grid=(bsz, per_b),
        in_specs=[
            pl.BlockSpec((tt, W_A), row),
            pl.BlockSpec((tt, W_A), row),
            pl.BlockSpec((CONV_W, W_A), const),
            pl.BlockSpec((1, W_A), const),
            pl.BlockSpec((W_A, 2 * W_A), const),
            pl.BlockSpec((1, 2 * W_A), const),
            pl.BlockSpec((1, W_A), const),
        ],
        out_specs=pl.BlockSpec((tt, W_A), row),
        scratch_shapes=[pltpu.VMEM((tt + SUBLANES, W_A), F32), pltpu.VMEM((SUBLANES, W_A), F32)],
        compiler_params=_cparams(("arbitrary", "arbitrary")),
        name="rglru",
    )(xa, ga, conv_w, conv_b.reshape(1, W_A), wg_bd, bg.reshape(1, 2 * W_A), lam.reshape(1, W_A))


def _gla_body(q_ref, k_ref, v_ref, og_ref, gl_ref, wg_ref, bg_ref, gn_ref, o_ref, st_ref):
    t = pl.program_id(1)
    tt = q_ref.shape[0]

    @pl.when(t == 0)
    def _():
        st_ref[...] = jnp.zeros_like(st_ref)

    ri = lax.broadcasted_iota(I32, (CHUNK, CHUNK), 0)
    ci = lax.broadcasted_iota(I32, (CHUNK, CHUNK), 1)
    tri_incl = (ri >= ci).astype(F32)

    def chunk(c, carry):
        r0 = pl.multiple_of(c * CHUNK, CHUNK)
        rows = pl.ds(r0, CHUNK)
        x = _dot_hi(gl_ref[rows, :], wg_ref[...]) + bg_ref[...]
        lg = -_softplus(-x) * (1.0 / GATE_NORM)
        g = _dot_hi(tri_incl, lg)
        eg = jnp.exp(g)
        q_all = q_ref[rows, :] * (DK_B ** -0.5)
        k_all = k_ref[rows, :]
        v_all = v_ref[rows, :]
        og_all = og_ref[rows, :]
        outs = []
        for h in range(NH_B):
            ks = slice(h * DKP_B, (h + 1) * DKP_B)
            vs = slice(h * DV_B, (h + 1) * DV_B)
            qh, kh, vh, gh = q_all[:, ks], k_all[:, ks], v_all[:, vs], g[:, ks]
            st = st_ref[h]
            o_inter = _dot_nt(qh * eg[:, ks], st)
            parts = []
            for blk in range(CHUNK // SUB):
                lo, hi = blk * SUB, (blk + 1) * SUB
                ref_g = gh[lo:lo + 1, :]
                qb = qh[lo:hi] * jnp.exp(gh[lo:hi] - ref_g)
                kb = kh[:hi] * jnp.exp(ref_g - gh[:hi])
                att = _dot_nt(qb, kb)
                causal = (lax.broadcasted_iota(I32, (SUB, hi), 1)
                          <= lax.broadcasted_iota(I32, (SUB, hi), 0) + lo)
                att = jnp.where(causal, att, 0.0)
                parts.append(_dot(att, vh[:hi]))
            o = o_inter + jnp.concatenate(parts, axis=0)
            g_last = gh[CHUNK - 1:CHUNK, :]
            kdec = kh * jnp.exp(g_last - gh)
            st_ref[h] = st * jnp.exp(g_last) + _dot_tn(vh, kdec)
            o = o * lax.rsqrt(jnp.mean(o * o, axis=-1, keepdims=True) + EPS) * gn_ref[...]
            outs.append(o * _silu(og_all[:, vs]))
        o_ref[rows, :] = jnp.concatenate(outs, axis=1)
        return carry

    lax.fori_loop(0, tt // CHUNK, chunk, 0)


def _gla(q, k, v, og, gl, wg2p, bg2p, gnorm, bsz, seq):
    tt = min(TT_SEQ, seq)
    per_b = seq // tt
    row = lambda b, t: (b * per_b + t, 0)
    const = lambda b, t: (0, 0)
    return pl.pallas_call(
        _gla_body,
        out_shape=jax.ShapeDtypeStruct(v.shape, F32),
        grid=(bsz, per_b),
        in_specs=[
            pl.BlockSpec((tt, KP_B), row),
            pl.BlockSpec((tt, KP_B), row),
            pl.BlockSpec((tt, V_B), row),
            pl.BlockSpec((tt, V_B), row),
            pl.BlockSpec((tt, LANES), row),
            pl.BlockSpec((LANES, KP_B), const),
            pl.BlockSpec((1, KP_B), const),
            pl.BlockSpec((1, DV_B), const),
        ],
        out_specs=pl.BlockSpec((tt, V_B), row),
        scratch_shapes=[pltpu.VMEM((NH_B, DV_B, DKP_B), F32)],
        compiler_params=_cparams(("arbitrary", "arbitrary")),
        name="gla",
    )(q, k, v, og, gl, wg2p, bg2p, gnorm.reshape(1, DV_B))


def _unit_lower_inverse_apply(low, rhs, ri, ci):
    same = (ri // SUB) == (ci // SUB)
    dg = jnp.where(same, low, 0.0)
    off = low - dg
    eye = (ri == ci).astype(F32)
    tinv = eye - dg
    pw = dg
    e = 2
    while e < SUB:
        pw = _dot(pw, pw)
        tinv = tinv + _dot(tinv, pw)
        e *= 2
    nmat = _dot(tinv, off)
    x = _dot(tinv, rhs)
    blocks = [x[0:SUB]]
    for blk in range(1, CHUNK // SUB):
        lo, hi = blk * SUB, (blk + 1) * SUB
        prev = jnp.concatenate(blocks, axis=0)
        blocks.append(x[lo:hi] - _dot(nmat[lo:hi, :lo], prev))
    return jnp.concatenate(blocks, axis=0)


def _delta_body(q_ref, k_ref, v_ref, z_ref, ba_ref, cq_ref, ck_ref, cv_ref, alog_ref, dtb_ref, dn_ref,
                o_ref, qbuf, kbuf, vbuf, qs, ks, vs, st_ref):
    t = pl.program_id(1)
    tt = q_ref.shape[0]

    @pl.when(t == 0)
    def _():
        st_ref[...] = jnp.zeros_like(st_ref)

    qs[...] = _silu(_causal_conv(q_ref, cq_ref, qbuf, t == 0))
    ks[...] = _silu(_causal_conv(k_ref, ck_ref, kbuf, t == 0))
    vs[...] = _silu(_causal_conv(v_ref, cv_ref, vbuf, t == 0))

    ri = lax.broadcasted_iota(I32, (CHUNK, CHUNK), 0)
    ci = lax.broadcasted_iota(I32, (CHUNK, CHUNK), 1)
    tri_incl = (ri >= ci).astype(F32)
    incl = ri >= ci
    strict = ri > ci

    def chunk(c, carry):
        r0 = pl.multiple_of(c * CHUNK, CHUNK)
        rows = pl.ds(r0, CHUNK)
        ba = ba_ref[rows, :]
        beta_all = _sigmoid(ba)
        g_all = -jnp.exp(alog_ref[...]) * _softplus(ba + dtb_ref[...])
        gcum = _dot_hi(tri_incl, g_all)
        gcum_t = gcum.T
        q_all, k_all, v_all, z_all = qs[rows, :], ks[rows, :], vs[rows, :], z_ref[rows, :]
        outs = []
        for h in range(NH_C):
            hs = slice(h * DH_C, (h + 1) * DH_C)
            qh, kh, vh = q_all[:, hs], k_all[:, hs], v_all[:, hs]
            qh = qh * lax.rsqrt(jnp.sum(qh * qh, axis=-1, keepdims=True) + EPS) * (DH_C ** -0.5)
            kh = kh * lax.rsqrt(jnp.sum(kh * kh, axis=-1, keepdims=True) + EPS)
            beta = beta_all[:, h:h + 1]
            gcol = gcum[:, NH_C + h:NH_C + h + 1]
            grow = gcum_t[NH_C + h:NH_C + h + 1, :]
            g_last = gcum[CHUNK - 1:CHUNK, NH_C + h:NH_C + h + 1]
            decay = jnp.exp(jnp.where(incl, gcol - grow, -jnp.inf))
            kbeta = kh * beta
            low = jnp.where(strict, _dot_nt(kbeta, kh) * decay, 0.0)
            rhs = jnp.concatenate([vh * beta, kbeta * jnp.exp(gcol)], axis=1)
            sol = _unit_lower_inverse_apply(low, rhs, ri, ci)
            u_c, w_c = sol[:, :DH_C], sol[:, DH_C:]
            st = st_ref[h]
            v_new = u_c - _dot(w_c, st)
            att = _dot_nt(qh, kh) * decay
            o = _dot(qh * jnp.exp(gcol), st) + _dot(att, v_new)
            st_ref[h] = st * jnp.exp(g_last) + _dot_tn(kh * jnp.exp(g_last - gcol), v_new)
            o = o * lax.rsqrt(jnp.mean(o * o, axis=-1, keepdims=True) + EPS) * dn_ref[...]
            outs.append(o * _silu(z_all[:, hs]))
        o_ref[rows, :] = jnp.concatenate(outs, axis=1)
        return carry

    lax.fori_loop(0, tt // CHUNK, chunk, 0)


def _delta(q, k, v, z, ba, cq, ck, cv, alog, dtb, dnorm, bsz, seq):
    tt = min(TT_SEQ, seq)
    per_b = seq // tt
    row = lambda b, t: (b * per_b + t, 0)
    const = lambda b, t: (0, 0)
    big = pl.BlockSpec((tt, W_C), row)
    cw = pl.BlockSpec((CONV_W, W_C), const)
    vec = pl.BlockSpec((1, LANES), const)
    return pl.pallas_call(
        _delta_body,
        out_shape=jax.ShapeDtypeStruct(q.shape, F32),
        grid=(bsz, per_b),
        in_specs=[big, big, big, big, pl.BlockSpec((tt, LANES), row), cw, cw, cw, vec, vec, vec],
        out_specs=big,
        scratch_shapes=[pltpu.VMEM((tt + SUBLANES, W_C), F32)] * 3 + [pltpu.VMEM((tt, W_C), F32)] * 3
        + [pltpu.VMEM((NH_C, DH_C, DH_C), F32)],
        compiler_params=_cparams(("arbitrary", "arbitrary")),
        name="delta",
    )(q, k, v, z, ba, cq, ck, cv, alog, dtb, dnorm)


def _outproj_router_body(n_parts, *refs):
    y_refs = refs[:n_parts]
    w_refs = refs[n_parts:2 * n_parts]
    (h_ref, mod_ref, g2_ref, wgrp_ref, bgrp_ref, wrt_ref, brt_ref,
     h1_ref, u2_ref, info_ref, cnt_ref, carry) = refs[2 * n_parts:]
    i = pl.program_id(0)
    tm = h_ref.shape[0]

    @pl.when(i == 0)
    def _():
        carry[...] = jnp.zeros_like(carry)

    m = None
    for y_ref, w_ref in zip(y_refs, w_refs):
        part = _dot(y_ref[...], w_ref[...])
        m = part if m is None else m + part
    h1 = h_ref[...] + mod_ref[2:3, :] * m
    h1_ref[...] = h1
    u2 = _rms_mod(h1, g2_ref[...], mod_ref[4:5, :], mod_ref[3:4, :])
    u2_ref[...] = u2

    lane = lax.broadcasted_iota(I32, (tm, LANES), 1)
    neg = -jnp.inf
    lgp = _dot_hi(u2, wgrp_ref[...]) + bgrp_ref[...]
    lgp = jnp.where(lane < N_GROUPS, lgp, neg)
    gmax = jnp.max(lgp, axis=-1, keepdims=True)
    pg_top = 1.0 / jnp.sum(jnp.exp(lgp - gmax), axis=-1, keepdims=True)
    g_idx = jnp.min(jnp.where(lgp == gmax, lane, LANES), axis=-1, keepdims=True)
    le = _dot_hi(u2, wrt_ref[...]) + brt_ref[...]
    in_group = (lane // EPG == g_idx) & (lane < N_EXP)
    le = jnp.where(in_group, le, neg)
    emax = jnp.max(le, axis=-1, keepdims=True)
    pe = jnp.exp(le - emax)
    pe = jnp.where(in_group, pe / jnp.sum(pe, axis=-1, keepdims=True), -1.0)
    p0 = jnp.max(pe, axis=-1, keepdims=True)
    e0 = jnp.min(jnp.where(pe == p0, lane, LANES), axis=-1, keepdims=True)
    pe1 = jnp.where(lane == e0, -1.0, pe)
    p1 = jnp.max(pe1, axis=-1, keepdims=True)
    e1 = jnp.min(jnp.where(pe1 == p1, lane, LANES), axis=-1, keepdims=True)
    den = p0 + p1
    w0 = pg_top * p0 / den
    w1 = pg_top * p1 / den
    hot = ((lane == e0) | (lane == e1)).astype(F32)
    ri = lax.broadcasted_iota(I32, (tm, tm), 0)
    ci = lax.broadcasted_iota(I32, (tm, tm), 1)
    before = _dot((ri > ci).astype(F32), hot) + carry[0:1, :]
    r0 = jnp.sum(jnp.where(lane == e0, before, 0.0), axis=-1, keepdims=True)
    r1 = jnp.sum(jnp.where(lane == e1, before, 0.0), axis=-1, keepdims=True)
    carry[...] = carry[...] + jnp.sum(hot, axis=0, keepdims=True)
    cnt_ref[...] = carry[...]
    info = jnp.where(lane == 0, e0.astype(F32), 0.0)
    info = jnp.where(lane == 1, e1.astype(F32), info)
    info = jnp.where(lane == 2, r0, info)
    info = jnp.where(lane == 3, r1, info)
    info = jnp.where(lane == 4, w0, info)
    info = jnp.where(lane == 5, w1, info)
    info_ref[...] = info


def _outproj_router(ys, ws, h, mod, g2, wgrp, bgrp, wrt, brt, seq):
    t_rows = h.shape[0]
    tm = min(TM_PROJ, seq)
    per_b = seq // tm
    n = len(ys)
    row = lambda i: (i, 0)
    const = lambda i: (0, 0)
    in_specs = [pl.BlockSpec((tm, y.shape[1]), row) for y in ys]
    in_specs += [pl.BlockSpec(w.shape, const) for w in ws]
    in_specs += [
        pl.BlockSpec((tm, D), row),
        pl.BlockSpec((None, 6, D), lambda i: (i // per_b, 0, 0)),
        pl.BlockSpec((1, D), const),
        pl.BlockSpec((D, LANES), const),
        pl.BlockSpec((1, LANES), const),
        pl.BlockSpec((D, LANES), const),
        pl.BlockSpec((1, LANES), const),
    ]
    return pl.pallas_call(
        functools.partial(_outproj_router_body, n),
        out_shape=[
            jax.ShapeDtypeStruct((t_rows, D), F32),
            jax.ShapeDtypeStruct((t_rows, D), F32),
            jax.ShapeDtypeStruct((t_rows, LANES), F32),
            jax.ShapeDtypeStruct((SUBLANES, LANES), F32),
        ],
        grid=(t_rows // tm,),
        in_specs=in_specs,
        out_specs=[
            pl.BlockSpec((tm, D), row),
            pl.BlockSpec((tm, D), row),
            pl.BlockSpec((tm, LANES), row),
            pl.BlockSpec((SUBLANES, LANES), const),
        ],
        scratch_shapes=[pltpu.VMEM((SUBLANES, LANES), F32)],
        compiler_params=_cparams(("arbitrary",)),
        name="outproj_router",
    )(*ys, *ws, h, mod, g2.reshape(1, D), wgrp, bgrp, wrt, brt)


def _row_copy(src, src_row, dst, dst_row, sem):
    return pltpu.make_async_copy(src.at[pl.ds(src_row, 1)], dst.at[pl.ds(dst_row, 1)], sem)


def _dispatch_body(pos_ref, u_hbm, xb_in, xb_out, sem):
    del xb_in
    i = pl.program_id(0)
    td = pos_ref.shape[2] // TOP_K
    base = i * td

    def issue(j, c):
        for k in range(TOP_K):
            _row_copy(u_hbm, base + j, xb_out, pos_ref[0, 0, TOP_K * j + k], sem).start()
        return c

    lax.fori_loop(0, td, issue, 0)

    def drain(j, c):
        for k in range(TOP_K):
            _row_copy(u_hbm, base + j, xb_out, pos_ref[0, 0, TOP_K * j + k], sem).wait()
        return c

    lax.fori_loop(0, td, drain, 0)


def _dispatch(pos, u2, n_rows):
    t_rows = u2.shape[0]
    td = min(TD_MOE, t_rows)
    nstep = t_rows // td
    xb0 = jnp.zeros((n_rows, D), F32)
    return pl.pallas_call(
        _dispatch_body,
        out_shape=jax.ShapeDtypeStruct((n_rows, D), F32),
        grid=(nstep,),
        in_specs=[
            pl.BlockSpec((1, 1, TOP_K * td), lambda i: (i, 0, 0), memory_space=pltpu.SMEM),
            pl.BlockSpec(memory_space=pl.ANY),
            pl.BlockSpec(memory_space=pl.ANY),
        ],
        out_specs=pl.BlockSpec(memory_space=pl.ANY),
        scratch_shapes=[pltpu.SemaphoreType.DMA(())],
        input_output_aliases={2: 0},
        compiler_params=_cparams(("arbitrary",)),
        name="moe_dispatch",
    )(pos.reshape(nstep, 1, TOP_K * td), u2, xb0)


def _ffn_body(be_ref, nu_ref, x_ref, w1_ref, w3_ref, w2_ref, o_ref, w13_s, w2_s):
    i = pl.program_id(0)
    prev = be_ref[jnp.maximum(i - 1, 0)]
    fresh = (i == 0) | (be_ref[i] != prev)

    @pl.when(fresh & (i < nu_ref[0]))
    def _():
        w13_s[:, :D_EXP] = w1_ref[...].astype(BF16)
        w13_s[:, D_EXP:] = w3_ref[...].astype(BF16)
        w2_s[...] = w2_ref[...].astype(BF16)

    @pl.when(i < nu_ref[0])
    def _():
        x = x_ref[...].astype(BF16)
        a = jnp.dot(x, w13_s[...], preferred_element_type=F32)
        hid = _silu(a[:, :D_EXP]) * a[:, D_EXP:]
        o_ref[...] = jnp.dot(hid.astype(BF16), w2_s[...], preferred_element_type=F32)

    @pl.when(i >= nu_ref[0])
    def _():
        o_ref[...] = jnp.zeros_like(o_ref)


def _ffn(blk_e, n_used, xb, w1, w3, w2):
    n_rows = xb.shape[0]
    nb = n_rows // BM_MOE

    def xmap(i, be, nu):
        return (jnp.minimum(i, nu[0] - 1), 0)

    def wmap(i, be, nu):
        return (be[jnp.minimum(i, nu[0] - 1)], 0, 0)

    return pl.pallas_call(
        _ffn_body,
        out_shape=jax.ShapeDtypeStruct((n_rows, D), F32),
        grid_spec=pltpu.PrefetchScalarGridSpec(
            num_scalar_prefetch=2,
            grid=(nb,),
            in_specs=[
                pl.BlockSpec((BM_MOE, D), xmap),
                pl.BlockSpec((None, D, D_EXP), wmap),
                pl.BlockSpec((None, D, D_EXP), wmap),
                pl.BlockSpec((None, D_EXP, D), wmap),
            ],
            out_specs=pl.BlockSpec((BM_MOE, D), lambda i, be, nu: (i, 0)),
            scratch_shapes=[pltpu.VMEM((D, 2 * D_EXP), BF16), pltpu.VMEM((D_EXP, D), BF16)],
        ),
        compiler_params=_cparams(("arbitrary",)),
        name="moe_ffn",
    )(blk_e, n_used, xb, w1, w3, w2)


def _combine_body(final, pos_ref, yb_hbm, h_ref, info_ref, mod_ref, fn_ref, o_ref, ybuf, sem):
    td = h_ref.shape[0]

    def issue(j, c):
        for k in range(TOP_K):
            _row_copy(yb_hbm, pos_ref[0, 0, TOP_K * j + k], ybuf.at[k], j, sem).start()
        return c

    lax.fori_loop(0, td, issue, 0)

    def drain(j, c):
        for k in range(TOP_K):
            _row_copy(yb_hbm, pos_ref[0, 0, TOP_K * j + k], ybuf.at[k], j, sem).wait()
        return c

    lax.fori_loop(0, td, drain, 0)
    info = info_ref[...]
    y = info[:, 4:5] * ybuf[0] + info[:, 5:6] * ybuf[1]
    h2 = h_ref[...] + mod_ref[5:6, :] * y
    if final:
        h2 = h2 * lax.rsqrt(jnp.mean(h2 * h2, axis=-1, keepdims=True) + EPS) * fn_ref[...]
    o_ref[...] = h2


def _combine(pos, yb, h1, info, mod, fnorm, seq, final):
    t_rows = h1.shape[0]
    td = min(TD_MOE, seq)
    per_b = seq // td
    nstep = t_rows // td
    row = lambda i: (i, 0)
    return pl.pallas_call(
        functools.partial(_combine_body, final),
        out_shape=jax.ShapeDtypeStruct((t_rows, D), F32),
        grid=(nstep,),
        in_specs=[
            pl.BlockSpec((1, 1, TOP_K * td), lambda i: (i, 0, 0), memory_space=pltpu.SMEM),
            pl.BlockSpec(memory_space=pl.ANY),
            pl.BlockSpec((td, D), row),
            pl.BlockSpec((td, LANES), row),
            pl.BlockSpec((None, 6, D), lambda i: (i // per_b, 0, 0)),
            pl.BlockSpec((1, D), lambda i: (0, 0)),
        ],
        out_specs=pl.BlockSpec((td, D), row),
        scratch_shapes=[pltpu.VMEM((TOP_K, td, D), F32), pltpu.SemaphoreType.DMA(())],
        compiler_params=_cparams(("arbitrary",)),
        name="moe_combine",
    )(pos.reshape(nstep, 1, TOP_K * td), yb, h1, info, mod, fnorm.reshape(1, D))


def _moe(u2, info, counts, h1, mod, w1, w3, w2, fnorm, seq, final):
    t_rows = u2.shape[0]
    nb = (t_rows * TOP_K) // BM_MOE + N_EXP
    n_rows = nb * BM_MOE
    cnt = counts[0, :N_EXP].astype(I32)
    padded = (cnt + BM_MOE - 1) // BM_MOE * BM_MOE
    end_p = jnp.cumsum(padded)
    start_p = end_p - padded
    eid = info[:, 0:TOP_K].astype(I32)
    rank = info[:, 2:2 + TOP_K].astype(I32)
    pos = (start_p[eid] + rank).reshape(t_rows * TOP_K)
    blk_e = jnp.minimum(jnp.sum(jnp.arange(nb, dtype=I32)[:, None] * BM_MOE >= end_p[None, :], axis=1),
                        N_EXP - 1).astype(I32)
    n_used = (end_p[-1:] // BM_MOE).astype(I32)
    xb = _dispatch(pos, u2, n_rows)
    yb = _ffn(blk_e, n_used, xb, w1, w3, w2)
    return _combine(pos, yb, h1, info, mod, fnorm, seq, final)


def _pad_cols(w, n):
    return jnp.pad(w, ((0, 0), (0, n - w.shape[1])))


def _pad_heads(w):
    w = w.reshape(w.shape[:-1] + (NH_B, DK_B))
    w = jnp.pad(w, [(0, 0)] * (w.ndim - 1) + [(0, DKP_B - DK_B)])
    return w.reshape(w.shape[:-2] + (KP_B,))


def _block_diag(w):
    nh, hw, _ = w.shape
    eye = jnp.eye(nh, dtype=w.dtype)
    return (eye[:, None, :, None] * w[:, :, None, :]).reshape(nh * hw, nh * hw)


def _router_weights(w_grp, b_grp, w_rt, b_rt):
    return (_pad_cols(w_grp, LANES), _pad_cols(b_grp.reshape(1, -1), LANES),
            _pad_cols(w_rt, LANES), _pad_cols(b_rt.reshape(1, -1), LANES))


def kernel(x, c, norm1, norm2, w_ada, b_ada, w_in_ab, conv_a_w, conv_a_b, rg_wa, rg_ba, rg_wx, rg_bx, rg_lam, gla_wg2, gla_bg2, gla_norm, w_out_ab, w_in_c, conv_c_w, dn_a_log, dn_dt_bias, dn_norm, w_out_c, moe_w_grp, moe_b_grp, moe_w_rt, moe_b_rt, moe_w1, moe_w3, moe_w2, final_norm):
    bsz, seq, _ = x.shape
    depth = w_ada.shape[0]
    t_rows = bsz * seq
    mod_all = _ada_mod(c, w_ada, b_ada)
    h = x.reshape(t_rows, D)
    for layer in range(depth):
        mod = mod_all[layer]
        j = layer // 2
        if layer % 2 == 0:
            wi = w_in_ab[j]
            o_q, o_k, o_v, o_gl = 2 * W_A, 2 * W_A + K_B, 2 * W_A + 2 * K_B, 2 * W_A + 2 * K_B + 2 * V_B
            w_in = jnp.concatenate([
                wi[:, :o_q], _pad_heads(wi[:, o_q:o_k]), _pad_heads(wi[:, o_k:o_v]), wi[:, o_v:o_gl],
                _pad_cols(wi[:, o_gl:], LANES)], axis=1).astype(BF16)
            xa, ga, q, k, v, og, gl = _inproj(h, mod, norm1[layer], w_in,
                                              (W_A, W_A, KP_B, KP_B, V_B, V_B, LANES), seq)
            wg_bd = jnp.concatenate([_block_diag(rg_wa[j]), _block_diag(rg_wx[j])], axis=1).astype(BF16)
            bg = jnp.concatenate([rg_ba[j], rg_bx[j]])
            ya = _rglru(xa, ga, conv_a_w[j], conv_a_b[j], wg_bd, bg, rg_lam[j], bsz, seq)
            wg2p = jnp.pad(_pad_heads(gla_wg2[j]), ((0, LANES - R_GATE), (0, 0)))
            bg2p = _pad_heads(gla_bg2[j].reshape(1, K_B))
            ob = _gla(q, k, v, og, gl, wg2p, bg2p, gla_norm[j], bsz, seq)
            ys = (ya, ob)
            ws = (w_out_ab[j][:W_A].astype(BF16), w_out_ab[j][W_A:].astype(BF16))
        else:
            p_c = w_in_c.shape[2]
            w_in = _pad_cols(w_in_c[j], p_c - 2 * NH_C + LANES).astype(BF16)
            q, k, v, z, ba = _inproj(h, mod, norm1[layer], w_in, (W_C, W_C, W_C, W_C, LANES), seq)
            cw = conv_c_w[j]
            lanes_c = jnp.zeros((1, LANES), F32)
            alog = lanes_c.at[0, NH_C:2 * NH_C].set(dn_a_log[j])
            dtb = lanes_c.at[0, NH_C:2 * NH_C].set(dn_dt_bias[j])
            o = _delta(q, k, v, z, ba, cw[:, :W_C], cw[:, W_C:2 * W_C], cw[:, 2 * W_C:], alog, dtb,
                       dn_norm[j].reshape(1, DH_C), bsz, seq)
            ys = (o,)
            ws = (w_out_c[j].astype(BF16),)
        rw = _router_weights(moe_w_grp[layer], moe_b_grp[layer], moe_w_rt[layer], moe_b_rt[layer])
        h1, u2, info, counts = _outproj_router(ys, ws, h, mod, norm2[layer], *rw, seq)
        h = _moe(u2, info, counts, h1, mod, moe_w1[layer], moe_w3[layer], moe_w2[layer], final_norm, seq,
                 final=layer == depth - 1)
    return h.reshape(bsz, seq, D)
```

```python
import functools

import jax
import jax.numpy as jnp
from jax import lax
from jax.experimental import pallas as pl
from jax.experimental.pallas import tpu as pltpu

F32 = jnp.float32
BF16 = jnp.bfloat16
I32 = jnp.int32
HI = lax.Precision.HIGHEST

D = 1024
CHUNK = 64
SUB = 16
CONV_W = 4
EPS = 1e-6
LANES = 128
SUBLANES = 8

W_A = D // 2
NH_A = 8
HW_A = W_A // NH_A
RG_C = 8.0
NH_B = 4
DK_B = 64
DV_B = 128
K_B = NH_B * DK_B
V_B = NH_B * DV_B
DKP_B = LANES
KP_B = NH_B * DKP_B
R_GATE = 16
GATE_NORM = 16.0

NH_C = 8
DH_C = 128
W_C = NH_C * DH_C

N_GROUPS = 4
EPG = 8
N_EXP = N_GROUPS * EPG
TOP_K = 2
D_EXP = 512

VMEM_LIMIT = 56 * 1024 * 1024

TM_PROJ = 512
TT_SEQ = 256
BM_MOE = 256
TD_MOE = 256


def _cparams(sem):
    return pltpu.CompilerParams(dimension_semantics=sem, vmem_limit_bytes=VMEM_LIMIT)


def _softplus(x):
    return jnp.maximum(x, 0.0) + jnp.log1p(jnp.exp(-jnp.abs(x)))


def _sigmoid(x):
    return jax.nn.sigmoid(x)


def _silu(x):
    return x * jax.nn.sigmoid(x)


def _rms_mod(h, g, sc, sh):
    y = h * lax.rsqrt(jnp.mean(h * h, axis=-1, keepdims=True) + EPS)
    return y * g * (1.0 + sc) + sh


def _dot(a, b):
    return jnp.dot(a.astype(BF16), b.astype(BF16), preferred_element_type=F32)


def _dot_nt(a, b):
    return lax.dot_general(a.astype(BF16), b.astype(BF16), (((1,), (1,)), ((), ())),
                           preferred_element_type=F32)


def _dot_tn(a, b):
    return lax.dot_general(a.astype(BF16), b.astype(BF16), (((0,), (0,)), ((), ())),
                           preferred_element_type=F32)


def _dot_hi(a, b):
    return jnp.dot(a, b, precision=HI, preferred_element_type=F32)


def _ada_body(c_ref, w_ref, b_ref, o_ref):
    o_ref[...] = _dot_hi(_silu(c_ref[...]), w_ref[...]) + b_ref[...]


def _ada_mod(c, w_ada, b_ada):
    depth, _, n6 = w_ada.shape
    bsz = c.shape[0]
    nt = 1536
    c8 = jnp.zeros((SUBLANES, D), F32).at[:bsz].set(c)
    out = pl.pallas_call(
        _ada_body,
        out_shape=jax.ShapeDtypeStruct((depth, SUBLANES, n6), F32),
        grid=(depth, n6 // nt),
        in_specs=[
            pl.BlockSpec((SUBLANES, D), lambda l, j: (0, 0)),
            pl.BlockSpec((None, D, nt), lambda l, j: (l, 0, j)),
            pl.BlockSpec((None, 1, nt), lambda l, j: (l, 0, j)),
        ],
        out_specs=pl.BlockSpec((None, SUBLANES, nt), lambda l, j: (l, 0, j)),
        compiler_params=_cparams(("arbitrary", "arbitrary")),
        name="ada_mod",
    )(c8, w_ada, b_ada.reshape(depth, 1, n6))
    return out[:, :bsz].reshape(depth, bsz, 6, D)


def _inproj_body(widths, h_ref, mod_ref, g_ref, w_ref, *out_refs):
    u = _rms_mod(h_ref[...], g_ref[...], mod_ref[1:2, :], mod_ref[0:1, :]).astype(BF16)
    off = 0
    for o_ref, wd in zip(out_refs, widths):
        o_ref[...] = jnp.dot(u, w_ref[:, off:off + wd], preferred_element_type=F32)
        off += wd


def _inproj(h, mod, g, w_bf16, widths, seq):
    t_rows = h.shape[0]
    tm = min(TM_PROJ, seq)
    per_b = seq // tm
    p = w_bf16.shape[1]
    assert sum(widths) == p
    return pl.pallas_call(
        functools.partial(_inproj_body, widths),
        out_shape=[jax.ShapeDtypeStruct((t_rows, wd), F32) for wd in widths],
        grid=(t_rows // tm,),
        in_specs=[
            pl.BlockSpec((tm, D), lambda i: (i, 0)),
            pl.BlockSpec((None, 6, D), lambda i: (i // per_b, 0, 0)),
            pl.BlockSpec((1, D), lambda i: (0, 0)),
            pl.BlockSpec((D, p), lambda i: (0, 0)),
        ],
        out_specs=[pl.BlockSpec((tm, wd), lambda i: (i, 0)) for wd in widths],
        compiler_params=_cparams(("arbitrary",)),
        name="inproj",
    )(h, mod, g.reshape(1, D), w_bf16)


def _causal_conv(x_ref, cw_ref, xbuf, first):
    tt = x_ref.shape[0]

    @pl.when(first)
    def _():
        xbuf[0:SUBLANES, :] = jnp.zeros((SUBLANES, xbuf.shape[1]), F32)

    xbuf[SUBLANES:SUBLANES + tt, :] = x_ref[...]
    acc = None
    for w in range(CONV_W):
        lo = SUBLANES - (CONV_W - 1) + w
        term = cw_ref[w:w + 1, :] * xbuf[lo:lo + tt, :]
        acc = term if acc is None else acc + term
    xbuf[0:SUBLANES, :] = xbuf[tt:tt + SUBLANES, :]
    return acc


def _rglru_body(xa_ref, ga_ref, cw_ref, cb_ref, wg_ref, bg_ref, lam_ref, o_ref, xbuf, hc):
    t = pl.program_id(1)
    tt = xa_ref.shape[0]

    @pl.when(t == 0)
    def _():
        hc[...] = jnp.zeros_like(hc)

    xc = _causal_conv(xa_ref, cw_ref, xbuf, t == 0) + cb_ref[...]
    gates = _dot(xc, wg_ref[...]) + bg_ref[...]
    r = _sigmoid(gates[:, :W_A])
    i = _sigmoid(gates[:, W_A:])
    log_a = -RG_C * r * _softplus(-lam_ref[...])
    a = jnp.exp(log_a)
    b = jnp.sqrt(1.0 - a * a) * (i * xc)
    row = lax.broadcasted_iota(I32, (tt, W_A), 0)
    s = 1
    while s < tt:
        a_sh = pltpu.roll(a, s, 0)
        b_sh = pltpu.roll(b, s, 0)
        valid = row >= s
        b = jnp.where(valid, a * b_sh + b, b)
        a = jnp.where(valid, a * a_sh, a)
        s *= 2
    h = b + a * hc[0:1, :]
    hc[...] = jnp.broadcast_to(h[tt - 1:tt, :], hc.shape)
    o_ref[...] = jax.nn.gelu(ga_ref[...], approximate=True) * h


def _rglru(xa, ga, conv_w, conv_b, wg_bd, bg, lam, bsz, seq):
    tt = min(TT_SEQ, seq)
    per_b = seq // tt
    row = lambda b, t: (b * per_b + t, 0)
    const = lambda b, t: (0, 0)
    return pl.pallas_call(
        _rglru_body,
        out_shape=jax.ShapeDtypeStruct(xa.shape, F32),
        grid=(bsz, per_b),
        in_specs=[
            pl.BlockSpec((tt, W_A), row),
            pl.BlockSpec((tt, W_A), row),
            pl.BlockSpec((CONV_W, W_A), const),
            pl.BlockSpec((1, W_A), const),
            pl.BlockSpec((W_A, 2 * W_A), const),
            pl.BlockSpec((1, 2 * W_A), const),
            pl.BlockSpec((1, W_A), const),
        ],
        out_specs=pl.BlockSpec((tt, W_A), row),
        scratch_shapes=[pltpu.VMEM((tt + SUBLANES, W_A), F32), pltpu.VMEM((SUBLANES, W_A), F32)],
        compiler_params=_cparams(("arbitrary", "arbitrary")),
        name="rglru",
    )(xa, ga, conv_w, conv_b.reshape(1, W_A), wg_bd, bg.reshape(1, 2 * W_A), lam.reshape(1, W_A))


def _gla_body(q_ref, k_ref, v_ref, og_ref, gl_ref, wg_ref, bg_ref, gn_ref, o_ref, st_ref):
    t = pl.program_id(1)
    tt = q_ref.shape[0]

    @pl.when(t == 0)
    def _():
        st_ref[...] = jnp.zeros_like(st_ref)

    ri = lax.broadcasted_iota(I32, (CHUNK, CHUNK), 0)
    ci = lax.broadcasted_iota(I32, (CHUNK, CHUNK), 1)
    tri_incl = (ri >= ci).astype(F32)

    def chunk(c, carry):
        r0 = pl.multiple_of(c * CHUNK, CHUNK)
        rows = pl.ds(r0, CHUNK)
        x = _dot_hi(gl_ref[rows, :], wg_ref[...]) + bg_ref[...]
        lg = -_softplus(-x) * (1.0 / GATE_NORM)
        g = _dot_hi(tri_incl, lg)
        eg = jnp.exp(g)
        q_all = q_ref[rows, :] * (DK_B ** -0.5)
        k_all = k_ref[rows, :]
        v_all = v_ref[rows, :]
        og_all = og_ref[rows, :]
        heads = range(NH_B)
        blocks = [(b * SUB, (b + 1) * SUB) for b in range(CHUNK // SUB)]
        qh = [q_all[:, h * DKP_B:(h + 1) * DKP_B] for h in heads]
        kh = [k_all[:, h * DKP_B:(h + 1) * DKP_B] for h in heads]
        gh = [g[:, h * DKP_B:(h + 1) * DKP_B] for h in heads]
        vh = [v_all[:, h * DV_B:(h + 1) * DV_B] for h in heads]
        st = [st_ref[h] for h in heads]
        g_last = [gh[h][CHUNK - 1:CHUNK, :] for h in heads]
        att = [[None] * len(blocks) for _ in heads]
        for h in heads:
            for b, (lo, hi) in enumerate(blocks):
                ref_g = gh[h][lo:lo + 1, :]
                qb = qh[h][lo:hi] * jnp.exp(gh[h][lo:hi] - ref_g)
                kb = kh[h][:hi] * jnp.exp(ref_g - gh[h][:hi])
                causal = (lax.broadcasted_iota(I32, (SUB, hi), 1)
                          <= lax.broadcasted_iota(I32, (SUB, hi), 0) + lo)
                att[h][b] = jnp.where(causal, _dot_nt(qb, kb), 0.0)
        o_inter = [_dot_nt(qh[h] * eg[:, h * DKP_B:(h + 1) * DKP_B], st[h]) for h in heads]
        upd = [_dot_tn(vh[h], kh[h] * jnp.exp(g_last[h] - gh[h])) for h in heads]
        o_intra = [[_dot(att[h][b], vh[h][:hi]) for b, (lo, hi) in enumerate(blocks)] for h in heads]
        outs = []
        for h in heads:
            st_ref[h] = st[h] * jnp.exp(g_last[h]) + upd[h]
            o = o_inter[h] + jnp.concatenate(o_intra[h], axis=0)
            o = o * lax.rsqrt(jnp.mean(o * o, axis=-1, keepdims=True) + EPS) * gn_ref[...]
            outs.append(o * _silu(og_all[:, h * DV_B:(h + 1) * DV_B]))
        o_ref[rows, :] = jnp.concatenate(outs, axis=1)
        return carry

    lax.fori_loop(0, tt // CHUNK, chunk, 0)


def _gla(q, k, v, og, gl, wg2p, bg2p, gnorm, bsz, seq):
    tt = min(TT_SEQ, seq)
    per_b = seq // tt
    row = lambda b, t: (b * per_b + t, 0)
    const = lambda b, t: (0, 0)
    return pl.pallas_call(
        _gla_body,
        out_shape=jax.ShapeDtypeStruct(v.shape, F32),
        grid=(bsz, per_b),
        in_specs=[
            pl.BlockSpec((tt, KP_B), row),
            pl.BlockSpec((tt, KP_B), row),
            pl.BlockSpec((tt, V_B), row),
            pl.BlockSpec((tt, V_B), row),
            pl.BlockSpec((tt, LANES), row),
            pl.BlockSpec((LANES, KP_B), const),
            pl.BlockSpec((1, KP_B), const),
            pl.BlockSpec((1, DV_B), const),
        ],
        out_specs=pl.BlockSpec((tt, V_B), row),
        scratch_shapes=[pltpu.VMEM((NH_B, DV_B, DKP_B), F32)],
        compiler_params=_cparams(("arbitrary", "arbitrary")),
        name="gla",
    )(q, k, v, og, gl, wg2p, bg2p, gnorm.reshape(1, DV_B))


def _unit_lower_solve(lows, rhss, ri, ci):
    n = range(len(lows))
    same = (ri // SUB) == (ci // SUB)
    eye = (ri == ci).astype(F32)
    dg = [jnp.where(same, lows[h], 0.0) for h in n]
    off = [lows[h] - dg[h] for h in n]
    tinv = [eye - dg[h] for h in n]
    pw = dg
    e = 2
    while e < SUB:
        pw = [_dot(pw[h], pw[h]) for h in n]
        tinv = [tinv[h] + _dot(tinv[h], pw[h]) for h in n]
        e *= 2
    nmat = [_dot(tinv[h], off[h]) for h in n]
    x = [_dot(tinv[h], rhss[h]) for h in n]
    sol = [[x[h][0:SUB]] for h in n]
    for blk in range(1, CHUNK // SUB):
        lo, hi = blk * SUB, (blk + 1) * SUB
        for h in n:
            prev = jnp.concatenate(sol[h], axis=0)
            sol[h].append(x[h][lo:hi] - _dot(nmat[h][lo:hi, :lo], prev))
    return [jnp.concatenate(sol[h], axis=0) for h in n]


def _delta_body(q_ref, k_ref, v_ref, z_ref, ba_ref, cq_ref, ck_ref, cv_ref, alog_ref, dtb_ref, dn_ref,
                o_ref, qbuf, kbuf, vbuf, qs, ks, vs, st_ref):
    t = pl.program_id(1)
    tt = q_ref.shape[0]

    @pl.when(t == 0)
    def _():
        st_ref[...] = jnp.zeros_like(st_ref)

    qs[...] = _silu(_causal_conv(q_ref, cq_ref, qbuf, t == 0))
    ks[...] = _silu(_causal_conv(k_ref, ck_ref, kbuf, t == 0))
    vs[...] = _silu(_causal_conv(v_ref, cv_ref, vbuf, t == 0))

    ri = lax.broadcasted_iota(I32, (CHUNK, CHUNK), 0)
    ci = lax.broadcasted_iota(I32, (CHUNK, CHUNK), 1)
    tri_incl = (ri >= ci).astype(F32)
    incl = ri >= ci
    strict = ri > ci

    def chunk(c, carry):
        r0 = pl.multiple_of(c * CHUNK, CHUNK)
        rows = pl.ds(r0, CHUNK)
        ba = ba_ref[rows, :]
        beta_all = _sigmoid(ba)
        g_all = -jnp.exp(alog_ref[...]) * _softplus(ba + dtb_ref[...])
        gcum = _dot_hi(tri_incl, g_all)
        gcum_t = gcum.T
        egc = jnp.exp(gcum)
        q_all, k_all, v_all, z_all = qs[rows, :], ks[rows, :], vs[rows, :], z_ref[rows, :]
        heads = range(NH_C)
        qh, kh, vh, beta, gcol, egcol, g_last, decay = [], [], [], [], [], [], [], []
        for h in heads:
            hs = slice(h * DH_C, (h + 1) * DH_C)
            q_, k_ = q_all[:, hs], k_all[:, hs]
            qh.append(q_ * lax.rsqrt(jnp.sum(q_ * q_, axis=-1, keepdims=True) + EPS) * (DH_C ** -0.5))
            kh.append(k_ * lax.rsqrt(jnp.sum(k_ * k_, axis=-1, keepdims=True) + EPS))
            vh.append(v_all[:, hs])
            beta.append(beta_all[:, h:h + 1])
            gcol.append(gcum[:, NH_C + h:NH_C + h + 1])
            egcol.append(egc[:, NH_C + h:NH_C + h + 1])
            g_last.append(gcum[CHUNK - 1:CHUNK, NH_C + h:NH_C + h + 1])
            grow = gcum_t[NH_C + h:NH_C + h + 1, :]
            decay.append(jnp.exp(jnp.where(incl, gcol[h] - grow, -jnp.inf)))
        kbeta = [kh[h] * beta[h] for h in heads]
        kk = [_dot_nt(kbeta[h], kh[h]) for h in heads]
        qk = [_dot_nt(qh[h], kh[h]) for h in heads]
        low = [jnp.where(strict, kk[h] * decay[h], 0.0) for h in heads]
        rhs = [jnp.concatenate([vh[h] * beta[h], kbeta[h] * egcol[h]], axis=1) for h in heads]
        sol = _unit_lower_solve(low, rhs, ri, ci)
        st = [st_ref[h] for h in heads]
        o_inter = [_dot(qh[h] * egcol[h], st[h]) for h in heads]
        v_new = [sol[h][:, :DH_C] - _dot(sol[h][:, DH_C:], st[h]) for h in heads]
        o_intra = [_dot(qk[h] * decay[h], v_new[h]) for h in heads]
        upd = [_dot_tn(kh[h] * jnp.exp(g_last[h] - gcol[h]), v_new[h]) for h in heads]
        outs = []
        for h in heads:
            st_ref[h] = st[h] * jnp.exp(g_last[h]) + upd[h]
            o = o_inter[h] + o_intra[h]
            o = o * lax.rsqrt(jnp.mean(o * o, axis=-1, keepdims=True) + EPS) * dn_ref[...]
            outs.append(o * _silu(z_all[:, h * DH_C:(h + 1) * DH_C]))
        o_ref[rows, :] = jnp.concatenate(outs, axis=1)
        return carry

    lax.fori_loop(0, tt // CHUNK, chunk, 0)


def _delta(q, k, v, z, ba, cq, ck, cv, alog, dtb, dnorm, bsz, seq):
    tt = min(TT_SEQ, seq)
    per_b = seq // tt
    row = lambda b, t: (b * per_b + t, 0)
    const = lambda b, t: (0, 0)
    big = pl.BlockSpec((tt, W_C), row)
    cw = pl.BlockSpec((CONV_W, W_C), const)
    vec = pl.BlockSpec((1, LANES), const)
    return pl.pallas_call(
        _delta_body,
        out_shape=jax.ShapeDtypeStruct(q.shape, F32),
        grid=(bsz, per_b),
        in_specs=[big, big, big, big, pl.BlockSpec((tt, LANES), row), cw, cw, cw, vec, vec, vec],
        out_specs=big,
        scratch_shapes=[pltpu.VMEM((tt + SUBLANES, W_C), F32)] * 3 + [pltpu.VMEM((tt, W_C), F32)] * 3
        + [pltpu.VMEM((NH_C, DH_C, DH_C), F32)],
        compiler_params=_cparams(("arbitrary", "arbitrary")),
        name="delta",
    )(q, k, v, z, ba, cq, ck, cv, alog, dtb, dnorm)


def _outproj_router_body(n_parts, *refs):
    y_refs = refs[:n_parts]
    w_refs = refs[n_parts:2 * n_parts]
    (h_ref, mod_ref, g2_ref, wgrp_ref, bgrp_ref, wrt_ref, brt_ref,
     h1_ref, u2_ref, info_ref, cnt_ref, carry) = refs[2 * n_parts:]
    i = pl.program_id(0)
    tm = h_ref.shape[0]

    @pl.when(i == 0)
    def _():
        carry[...] = jnp.zeros_like(carry)

    m = None
    for y_ref, w_ref in zip(y_refs, w_refs):
        part = _dot(y_ref[...], w_ref[...])
        m = part if m is None else m + part
    h1 = h_ref[...] + mod_ref[2:3, :] * m
    h1_ref[...] = h1
    u2 = _rms_mod(h1, g2_ref[...], mod_ref[4:5, :], mod_ref[3:4, :])
    u2_ref[...] = u2

    lane = lax.broadcasted_iota(I32, (tm, LANES), 1)
    neg = -jnp.inf
    lgp = _dot_hi(u2, wgrp_ref[...]) + bgrp_ref[...]
    lgp = jnp.where(lane < N_GROUPS, lgp, neg)
    gmax = jnp.max(lgp, axis=-1, keepdims=True)
    pg_top = 1.0 / jnp.sum(jnp.exp(lgp - gmax), axis=-1, keepdims=True)
    g_idx = jnp.min(jnp.where(lgp == gmax, lane, LANES), axis=-1, keepdims=True)
    le = _dot_hi(u2, wrt_ref[...]) + brt_ref[...]
    in_group = (lane // EPG == g_idx) & (lane < N_EXP)
    le = jnp.where(in_group, le, neg)
    emax = jnp.max(le, axis=-1, keepdims=True)
    pe = jnp.exp(le - emax)
    pe = jnp.where(in_group, pe / jnp.sum(pe, axis=-1, keepdims=True), -1.0)
    p0 = jnp.max(pe, axis=-1, keepdims=True)
    e0 = jnp.min(jnp.where(pe == p0, lane, LANES), axis=-1, keepdims=True)
    pe1 = jnp.where(lane == e0, -1.0, pe)
    p1 = jnp.max(pe1, axis=-1, keepdims=True)
    e1 = jnp.min(jnp.where(pe1 == p1, lane, LANES), axis=-1, keepdims=True)
    den = p0 + p1
    w0 = pg_top * p0 / den
    w1 = pg_top * p1 / den
    hot = ((lane == e0) | (lane == e1)).astype(F32)
    ri = lax.broadcasted_iota(I32, (tm, tm), 0)
    ci = lax.broadcasted_iota(I32, (tm, tm), 1)
    before = _dot((ri > ci).astype(F32), hot) + carry[0:1, :]
    r0 = jnp.sum(jnp.where(lane == e0, before, 0.0), axis=-1, keepdims=True)
    r1 = jnp.sum(jnp.where(lane == e1, before, 0.0), axis=-1, keepdims=True)
    carry[...] = carry[...] + jnp.sum(hot, axis=0, keepdims=True)
    cnt_ref[...] = carry[...]
    info = jnp.where(lane == 0, e0.astype(F32), 0.0)
    info = jnp.where(lane == 1, e1.astype(F32), info)
    info = jnp.where(lane == 2, r0, info)
    info = jnp.where(lane == 3, r1, info)
    info = jnp.where(lane == 4, w0, info)
    info = jnp.where(lane == 5, w1, info)
    info_ref[...] = info


def _outproj_router(ys, ws, h, mod, g2, wgrp, bgrp, wrt, brt, seq):
    t_rows = h.shape[0]
    tm = min(TM_PROJ, seq)
    per_b = seq // tm
    n = len(ys)
    row = lambda i: (i, 0)
    const = lambda i: (0, 0)
    in_specs = [pl.BlockSpec((tm, y.shape[1]), row) for y in ys]
    in_specs += [pl.BlockSpec(w.shape, const) for w in ws]
    in_specs += [
        pl.BlockSpec((tm, D), row),
        pl.BlockSpec((None, 6, D), lambda i: (i // per_b, 0, 0)),
        pl.BlockSpec((1, D), const),
        pl.BlockSpec((D, LANES), const),
        pl.BlockSpec((1, LANES), const),
        pl.BlockSpec((D, LANES), const),
        pl.BlockSpec((1, LANES), const),
    ]
    return pl.pallas_call(
        functools.partial(_outproj_router_body, n),
        out_shape=[
            jax.ShapeDtypeStruct((t_rows, D), F32),
            jax.ShapeDtypeStruct((t_rows, D), F32),
            jax.ShapeDtypeStruct((t_rows, LANES), F32),
            jax.ShapeDtypeStruct((SUBLANES, LANES), F32),
        ],
        grid=(t_rows // tm,),
        in_specs=in_specs,
        out_specs=[
            pl.BlockSpec((tm, D), row),
            pl.BlockSpec((tm, D), row),
            pl.BlockSpec((tm, LANES), row),
            pl.BlockSpec((SUBLANES, LANES), const),
        ],
        scratch_shapes=[pltpu.VMEM((SUBLANES, LANES), F32)],
        compiler_params=_cparams(("arbitrary",)),
        name="outproj_router",
    )(*ys, *ws, h, mod, g2.reshape(1, D), wgrp, bgrp, wrt, brt)


def _row_copy(src, src_row, dst, dst_row, sem):
    return pltpu.make_async_copy(src.at[pl.ds(src_row, 1)], dst.at[pl.ds(dst_row, 1)], sem)


def _dispatch_body(pos_ref, u_ref, xb_in, xb_out, sem):
    del xb_in
    td = u_ref.shape[0]

    def issue(j, c):
        for k in range(TOP_K):
            _row_copy(u_ref, j, xb_out, pos_ref[0, 0, TOP_K * j + k], sem).start()
        return c

    lax.fori_loop(0, td, issue, 0)

    def drain(j, c):
        for k in range(TOP_K):
            _row_copy(u_ref, j, xb_out, pos_ref[0, 0, TOP_K * j + k], sem).wait()
        return c

    lax.fori_loop(0, td, drain, 0)


def _dispatch(pos, u2, n_rows):
    t_rows = u2.shape[0]
    td = min(TD_MOE, t_rows)
    nstep = t_rows // td
    xb0 = jnp.zeros((n_rows, D), F32)
    return pl.pallas_call(
        _dispatch_body,
        out_shape=jax.ShapeDtypeStruct((n_rows, D), F32),
        grid=(nstep,),
        in_specs=[
            pl.BlockSpec((1, 1, TOP_K * td), lambda i: (i, 0, 0), memory_space=pltpu.SMEM),
            pl.BlockSpec((td, D), lambda i: (i, 0)),
            pl.BlockSpec(memory_space=pl.ANY),
        ],
        out_specs=pl.BlockSpec(memory_space=pl.ANY),
        scratch_shapes=[pltpu.SemaphoreType.DMA(())],
        input_output_aliases={2: 0},
        compiler_params=_cparams(("arbitrary",)),
        name="moe_dispatch",
    )(pos.reshape(nstep, 1, TOP_K * td), u2, xb0)


def _ffn_body(be_ref, nu_ref, x_ref, w1_ref, w3_ref, w2_ref, o_ref, w13_s, w2_s):
    i = pl.program_id(0)
    prev = be_ref[jnp.maximum(i - 1, 0)]
    fresh = (i == 0) | (be_ref[i] != prev)

    @pl.when(fresh & (i < nu_ref[0]))
    def _():
        w13_s[:, :D_EXP] = w1_ref[...].astype(BF16)
        w13_s[:, D_EXP:] = w3_ref[...].astype(BF16)
        w2_s[...] = w2_ref[...].astype(BF16)

    @pl.when(i < nu_ref[0])
    def _():
        x = x_ref[...].astype(BF16)
        a = jnp.dot(x, w13_s[...], preferred_element_type=F32)
        hid = _silu(a[:, :D_EXP]) * a[:, D_EXP:]
        o_ref[...] = jnp.dot(hid.astype(BF16), w2_s[...], preferred_element_type=F32)

    @pl.when(i >= nu_ref[0])
    def _():
        o_ref[...] = jnp.zeros_like(o_ref)


def _ffn(blk_e, n_used, xb, w1, w3, w2):
    n_rows = xb.shape[0]
    nb = n_rows // BM_MOE

    def xmap(i, be, nu):
        return (jnp.minimum(i, nu[0] - 1), 0)

    def wmap(i, be, nu):
        return (be[jnp.minimum(i, nu[0] - 1)], 0, 0)

    return pl.pallas_call(
        _ffn_body,
        out_shape=jax.ShapeDtypeStruct((n_rows, D), F32),
        grid_spec=pltpu.PrefetchScalarGridSpec(
            num_scalar_prefetch=2,
            grid=(nb,),
            in_specs=[
                pl.BlockSpec((BM_MOE, D), xmap),
                pl.BlockSpec((None, D, D_EXP), wmap),
                pl.BlockSpec((None, D, D_EXP), wmap),
                pl.BlockSpec((None, D_EXP, D), wmap),
            ],
            out_specs=pl.BlockSpec((BM_MOE, D), lambda i, be, nu: (i, 0)),
            scratch_shapes=[pltpu.VMEM((D, 2 * D_EXP), BF16), pltpu.VMEM((D_EXP, D), BF16)],
        ),
        compiler_params=_cparams(("arbitrary",)),
        name="moe_ffn",
    )(blk_e, n_used, xb, w1, w3, w2)


def _combine_body(final, pos_ref, yb_hbm, h_ref, info_ref, mod_ref, fn_ref, o_ref, ybuf, sem):
    td = h_ref.shape[0]

    def issue(j, c):
        for k in range(TOP_K):
            _row_copy(yb_hbm, pos_ref[0, 0, TOP_K * j + k], ybuf.at[k], j, sem).start()
        return c

    lax.fori_loop(0, td, issue, 0)

    def drain(j, c):
        for k in range(TOP_K):
            _row_copy(yb_hbm, pos_ref[0, 0, TOP_K * j + k], ybuf.at[k], j, sem).wait()
        return c

    lax.fori_loop(0, td, drain, 0)
    info = info_ref[...]
    y = info[:, 4:5] * ybuf[0] + info[:, 5:6] * ybuf[1]
    h2 = h_ref[...] + mod_ref[5:6, :] * y
    if final:
        h2 = h2 * lax.rsqrt(jnp.mean(h2 * h2, axis=-1, keepdims=True) + EPS) * fn_ref[...]
    o_ref[...] = h2


def _combine(pos, yb, h1, info, mod, fnorm, seq, final):
    t_rows = h1.shape[0]
    td = min(TD_MOE, seq)
    per_b = seq // td
    nstep = t_rows // td
    row = lambda i: (i, 0)
    return pl.pallas_call(
        functools.partial(_combine_body, final),
        out_shape=jax.ShapeDtypeStruct((t_rows, D), F32),
        grid=(nstep,),
        in_specs=[
            pl.BlockSpec((1, 1, TOP_K * td), lambda i: (i, 0, 0), memory_space=pltpu.SMEM),
            pl.BlockSpec(memory_space=pl.ANY),
            pl.BlockSpec((td, D), row),
            pl.BlockSpec((td, LANES), row),
            pl.BlockSpec((None, 6, D), lambda i: (i // per_b, 0, 0)),
            pl.BlockSpec((1, D), lambda i: (0, 0)),
        ],
        out_specs=pl.BlockSpec((td, D), row),
        scratch_shapes=[pltpu.VMEM((TOP_K, td, D), F32), pltpu.SemaphoreType.DMA(())],
        compiler_params=_cparams(("arbitrary",)),
        name="moe_combine",
    )(pos.reshape(nstep, 1, TOP_K * td), yb, h1, info, mod, fnorm.reshape(1, D))


def _moe(u2, info, counts, h1, mod, w1, w3, w2, fnorm, seq, final):
    t_rows = u2.shape[0]
    nb = (t_rows * TOP_K) // BM_MOE + N_EXP
    n_rows = nb * BM_MOE
    cnt = counts[0, :N_EXP].astype(I32)
    padded = (cnt + BM_MOE - 1) // BM_MOE * BM_MOE
    end_p = jnp.cumsum(padded)
    start_p = end_p - padded
    eid = info[:, 0:TOP_K].astype(I32)
    rank = info[:, 2:2 + TOP_K].astype(I32)
    pos = (start_p[eid] + rank).reshape(t_rows * TOP_K)
    blk_e = jnp.minimum(jnp.sum(jnp.arange(nb, dtype=I32)[:, None] * BM_MOE >= end_p[None, :], axis=1),
                        N_EXP - 1).astype(I32)
    n_used = (end_p[-1:] // BM_MOE).astype(I32)
    xb = _dispatch(pos, u2, n_rows)
    yb = _ffn(blk_e, n_used, xb, w1, w3, w2)
    return _combine(pos, yb, h1, info, mod, fnorm, seq, final)


def _pad_cols(w, n):
    return jnp.pad(w, ((0, 0), (0, n - w.shape[1])))


def _pad_heads(w):
    w = w.reshape(w.shape[:-1] + (NH_B, DK_B))
    w = jnp.pad(w, [(0, 0)] * (w.ndim - 1) + [(0, DKP_B - DK_B)])
    return w.reshape(w.shape[:-2] + (KP_B,))


def _block_diag(w):
    nh, hw, _ = w.shape
    eye = jnp.eye(nh, dtype=w.dtype)
    return (eye[:, None, :, None] * w[:, :, None, :]).reshape(nh * hw, nh * hw)


def _router_weights(w_grp, b_grp, w_rt, b_rt):
    return (_pad_cols(w_grp, LANES), _pad_cols(b_grp.reshape(1, -1), LANES),
            _pad_cols(w_rt, LANES), _pad_cols(b_rt.reshape(1, -1), LANES))


def kernel(x, c, norm1, norm2, w_ada, b_ada, w_in_ab, conv_a_w, conv_a_b, rg_wa, rg_ba, rg_wx, rg_bx, rg_lam, gla_wg2, gla_bg2, gla_norm, w_out_ab, w_in_c, conv_c_w, dn_a_log, dn_dt_bias, dn_norm, w_out_c, moe_w_grp, moe_b_grp, moe_w_rt, moe_b_rt, moe_w1, moe_w3, moe_w2, final_norm):
    bsz, seq, _ = x.shape
    depth = w_ada.shape[0]
    t_rows = bsz * seq
    mod_all = _ada_mod(c, w_ada, b_ada)
    h = x.reshape(t_rows, D)
    for layer in range(depth):
        mod = mod_all[layer]
        j = layer // 2
        if layer % 2 == 0:
            wi = w_in_ab[j]
            o_q, o_k, o_v, o_gl = 2 * W_A, 2 * W_A + K_B, 2 * W_A + 2 * K_B, 2 * W_A + 2 * K_B + 2 * V_B
            w_in = jnp.concatenate([
                wi[:, :o_q], _pad_heads(wi[:, o_q:o_k]), _pad_heads(wi[:, o_k:o_v]), wi[:, o_v:o_gl],
                _pad_cols(wi[:, o_gl:], LANES)], axis=1).astype(BF16)
            xa, ga, q, k, v, og, gl = _inproj(h, mod, norm1[layer], w_in,
                                              (W_A, W_A, KP_B, KP_B, V_B, V_B, LANES), seq)
            wg_bd = jnp.concatenate([_block_diag(rg_wa[j]), _block_diag(rg_wx[j])], axis=1).astype(BF16)
            bg = jnp.concatenate([rg_ba[j], rg_bx[j]])
            ya = _rglru(xa, ga, conv_a_w[j], conv_a_b[j], wg_bd, bg, rg_lam[j], bsz, seq)
            wg2p = jnp.pad(_pad_heads(gla_wg2[j]), ((0, LANES - R_GATE), (0, 0)))
            bg2p = _pad_heads(gla_bg2[j].reshape(1, K_B))
            ob = _gla(q, k, v, og, gl, wg2p, bg2p, gla_norm[j], bsz, seq)
            ys = (ya, ob)
            ws = (w_out_ab[j][:W_A].astype(BF16), w_out_ab[j][W_A:].astype(BF16))
        else:
            p_c = w_in_c.shape[2]
            w_in = _pad_cols(w_in_c[j], p_c - 2 * NH_C + LANES).astype(BF16)
            q, k, v, z, ba = _inproj(h, mod, norm1[layer], w_in, (W_C, W_C, W_C, W_C, LANES), seq)
            cw = conv_c_w[j]
            lanes_c = jnp.zeros((1, LANES), F32)
            alog = lanes_c.at[0, NH_C:2 * NH_C].set(dn_a_log[j])
            dtb = lanes_c.at[0, NH_C:2 * NH_C].set(dn_dt_bias[j])
            o = _delta(q, k, v, z, ba, cw[:, :W_C], cw[:, W_C:2 * W_C], cw[:, 2 * W_C:], alog, dtb,
                       dn_norm[j].reshape(1, DH_C), bsz, seq)
            ys = (o,)
            ws = (w_out_c[j].astype(BF16),)
        rw = _router_weights(moe_w_grp[layer], moe_b_grp[layer], moe_w_rt[layer], moe_b_rt[layer])
        h1, u2, info, counts = _outproj_router(ys, ws, h, mod, norm2[layer], *rw, seq)
        h = _moe(u2, info, counts, h1, mod, moe_w1[layer], moe_w3[layer], moe_w2[layer], final_norm, seq,
                 final=layer == depth - 1)
    return h.reshape(bsz, seq, D)
```

```python
import functools

import jax
import jax.numpy as jnp
from jax import lax
from jax.experimental import pallas as pl
from jax.experimental.pallas import tpu as pltpu

F32 = jnp.float32
BF16 = jnp.bfloat16
I32 = jnp.int32
HI = lax.Precision.HIGHEST

D = 1024
CHUNK = 64
SUB = 16
CONV_W = 4
EPS = 1e-6
LANES = 128
SUBLANES = 8

W_A = D // 2
NH_A = 8
HW_A = W_A // NH_A
RG_C = 8.0
NH_B = 4
DK_B = 64
DV_B = 128
K_B = NH_B * DK_B
V_B = NH_B * DV_B
DKP_B = LANES
KP_B = NH_B * DKP_B
R_GATE = 16
GATE_NORM = 16.0

NH_C = 8
DH_C = 128
W_C = NH_C * DH_C

N_GROUPS = 4
EPG = 8
N_EXP = N_GROUPS * EPG
TOP_K = 2
D_EXP = 512

VMEM_LIMIT = 56 * 1024 * 1024

TM_PROJ = 512
TT_SEQ = 256
BM_MOE = 256
TD_MOE = 256
DMA_UNROLL = 8


def _cparams(sem):
    return pltpu.CompilerParams(dimension_semantics=sem, vmem_limit_bytes=VMEM_LIMIT)


def _softplus(x):
    return jnp.maximum(x, 0.0) + jnp.log1p(jnp.exp(-jnp.abs(x)))


def _sigmoid(x):
    return jax.nn.sigmoid(x)


def _silu(x):
    return x * jax.nn.sigmoid(x)


def _rms_mod(h, g, sc, sh):
    y = h * lax.rsqrt(jnp.mean(h * h, axis=-1, keepdims=True) + EPS)
    return y * g * (1.0 + sc) + sh


def _dot(a, b):
    return jnp.dot(a.astype(BF16), b.astype(BF16), preferred_element_type=F32)


def _dot_nt(a, b):
    return lax.dot_general(a.astype(BF16), b.astype(BF16), (((1,), (1,)), ((), ())),
                           preferred_element_type=F32)


def _dot_tn(a, b):
    return lax.dot_general(a.astype(BF16), b.astype(BF16), (((0,), (0,)), ((), ())),
                           preferred_element_type=F32)


def _dot_hi(a, b):
    return jnp.dot(a, b, precision=HI, preferred_element_type=F32)


def _ada_body(c_ref, w_ref, b_ref, o_ref):
    o_ref[...] = _dot_hi(_silu(c_ref[...]), w_ref[...]) + b_ref[...]


def _ada_mod(c, w_ada, b_ada):
    depth, _, n6 = w_ada.shape
    bsz = c.shape[0]
    nt = 1536
    c8 = jnp.zeros((SUBLANES, D), F32).at[:bsz].set(c)
    out = pl.pallas_call(
        _ada_body,
        out_shape=jax.ShapeDtypeStruct((depth, SUBLANES, n6), F32),
        grid=(depth, n6 // nt),
        in_specs=[
            pl.BlockSpec((SUBLANES, D), lambda l, j: (0, 0)),
            pl.BlockSpec((None, D, nt), lambda l, j: (l, 0, j)),
            pl.BlockSpec((None, 1, nt), lambda l, j: (l, 0, j)),
        ],
        out_specs=pl.BlockSpec((None, SUBLANES, nt), lambda l, j: (l, 0, j)),
        compiler_params=_cparams(("arbitrary", "arbitrary")),
        name="ada_mod",
    )(c8, w_ada, b_ada.reshape(depth, 1, n6))
    return out[:, :bsz].reshape(depth, bsz, 6, D)


def _inproj_body(widths, h_ref, mod_ref, g_ref, w_ref, *out_refs):
    u = _rms_mod(h_ref[...], g_ref[...], mod_ref[1:2, :], mod_ref[0:1, :]).astype(BF16)
    off = 0
    for o_ref, wd in zip(out_refs, widths):
        o_ref[...] = jnp.dot(u, w_ref[:, off:off + wd], preferred_element_type=F32)
        off += wd


def _inproj(h, mod, g, w_bf16, widths, seq):
    t_rows = h.shape[0]
    tm = min(TM_PROJ, seq)
    per_b = seq // tm
    p = w_bf16.shape[1]
    assert sum(widths) == p
    return pl.pallas_call(
        functools.partial(_inproj_body, widths),
        out_shape=[jax.ShapeDtypeStruct((t_rows, wd), F32) for wd in widths],
        grid=(t_rows // tm,),
        in_specs=[
            pl.BlockSpec((tm, D), lambda i: (i, 0)),
            pl.BlockSpec((None, 6, D), lambda i: (i // per_b, 0, 0)),
            pl.BlockSpec((1, D), lambda i: (0, 0)),
            pl.BlockSpec((D, p), lambda i: (0, 0)),
        ],
        out_specs=[pl.BlockSpec((tm, wd), lambda i: (i, 0)) for wd in widths],
        compiler_params=_cparams(("arbitrary",)),
        name="inproj",
    )(h, mod, g.reshape(1, D), w_bf16)


def _causal_conv(x_ref, cw_ref, xbuf, first):
    tt = x_ref.shape[0]

    @pl.when(first)
    def _():
        xbuf[0:SUBLANES, :] = jnp.zeros((SUBLANES, xbuf.shape[1]), F32)

    xbuf[SUBLANES:SUBLANES + tt, :] = x_ref[...]
    acc = None
    for w in range(CONV_W):
        lo = SUBLANES - (CONV_W - 1) + w
        term = cw_ref[w:w + 1, :] * xbuf[lo:lo + tt, :]
        acc = term if acc is None else acc + term
    xbuf[0:SUBLANES, :] = xbuf[tt:tt + SUBLANES, :]
    return acc


def _rglru_body(xa_ref, ga_ref, cw_ref, cb_ref, wg_ref, bg_ref, lam_ref, o_ref, xbuf, hc):
    t = pl.program_id(1)
    tt = xa_ref.shape[0]

    @pl.when(t == 0)
    def _():
        hc[...] = jnp.zeros_like(hc)

    xc = _causal_conv(xa_ref, cw_ref, xbuf, t == 0) + cb_ref[...]
    gates = _dot(xc, wg_ref[...]) + bg_ref[...]
    r = _sigmoid(gates[:, :W_A])
    i = _sigmoid(gates[:, W_A:])
    log_a = -RG_C * r * _softplus(-lam_ref[...])
    a = jnp.exp(log_a)
    b = jnp.sqrt(1.0 - a * a) * (i * xc)
    row = lax.broadcasted_iota(I32, (tt, W_A), 0)
    s = 1
    while s < tt:
        a_sh = pltpu.roll(a, s, 0)
        b_sh = pltpu.roll(b, s, 0)
        valid = row >= s
        b = jnp.where(valid, a * b_sh + b, b)
        a = jnp.where(valid, a * a_sh, a)
        s *= 2
    h = b + a * hc[0:1, :]
    hc[...] = jnp.broadcast_to(h[tt - 1:tt, :], hc.shape)
    o_ref[...] = jax.nn.gelu(ga_ref[...], approximate=True) * h


def _rglru(xa, ga, conv_w, conv_b, wg_bd, bg, lam, bsz, seq):
    tt = min(TT_SEQ, seq)
    per_b = seq // tt
    row = lambda b, t: (b * per_b + t, 0)
    const = lambda b, t: (0, 0)
    return pl.pallas_call(
        _rglru_body,
        out_shape=jax.ShapeDtypeStruct(xa.shape, F32),
        grid=(bsz, per_b),
        in_specs=[
            pl.BlockSpec((tt, W_A), row),
            pl.BlockSpec((tt, W_A), row),
            pl.BlockSpec((CONV_W, W_A), const),
            pl.BlockSpec((1, W_A), const),
            pl.BlockSpec((W_A, 2 * W_A), const),
            pl.BlockSpec((1, 2 * W_A), const),
            pl.BlockSpec((1, W_A), const),
        ],
        out_specs=pl.BlockSpec((tt, W_A), row),
        scratch_shapes=[pltpu.VMEM((tt + SUBLANES, W_A), F32), pltpu.VMEM((SUBLANES, W_A), F32)],
        compiler_params=_cparams(("arbitrary", "arbitrary")),
        name="rglru",
    )(xa, ga, conv_w, conv_b.reshape(1, W_A), wg_bd, bg.reshape(1, 2 * W_A), lam.reshape(1, W_A))


def _gla_body(q_ref, k_ref, v_ref, og_ref, gl_ref, wg_ref, bg_ref, gn_ref, o_ref, st_ref):
    t = pl.program_id(0)
    nbat, tt = q_ref.shape[0], q_ref.shape[1]

    @pl.when(t == 0)
    def _():
        st_ref[...] = jnp.zeros_like(st_ref)

    ri = lax.broadcasted_iota(I32, (CHUNK, CHUNK), 0)
    ci = lax.broadcasted_iota(I32, (CHUNK, CHUNK), 1)
    tri_incl = (ri >= ci).astype(F32)

    def chunk(c, carry):
        r0 = pl.multiple_of(c * CHUNK, CHUNK)
        rows = pl.ds(r0, CHUNK)
        heads = range(nbat * NH_B)
        blocks = [(b * SUB, (b + 1) * SUB) for b in range(CHUNK // SUB)]
        qh, kh, gh, egh, vh, ogh = [], [], [], [], [], []
        for b in range(nbat):
            x = _dot_hi(gl_ref[b, rows, :], wg_ref[...]) + bg_ref[...]
            lg = -_softplus(-x) * (1.0 / GATE_NORM)
            g = _dot_hi(tri_incl, lg)
            eg = jnp.exp(g)
            q_all = q_ref[b, rows, :] * (DK_B ** -0.5)
            k_all = k_ref[b, rows, :]
            v_all = v_ref[b, rows, :]
            og_all = og_ref[b, rows, :]
            for hh in range(NH_B):
                ks_ = slice(hh * DKP_B, (hh + 1) * DKP_B)
                vs_ = slice(hh * DV_B, (hh + 1) * DV_B)
                qh.append(q_all[:, ks_])
                kh.append(k_all[:, ks_])
                gh.append(g[:, ks_])
                egh.append(eg[:, ks_])
                vh.append(v_all[:, vs_])
                ogh.append(og_all[:, vs_])
        st = [st_ref[h] for h in heads]
        g_last = [gh[h][CHUNK - 1:CHUNK, :] for h in heads]
        att = [[None] * len(blocks) for _ in heads]
        for h in heads:
            for b, (lo, hi) in enumerate(blocks):
                ref_g = gh[h][lo:lo + 1, :]
                qb = qh[h][lo:hi] * jnp.exp(gh[h][lo:hi] - ref_g)
                kb = kh[h][:hi] * jnp.exp(ref_g - gh[h][:hi])
                causal = (lax.broadcasted_iota(I32, (SUB, hi), 1)
                          <= lax.broadcasted_iota(I32, (SUB, hi), 0) + lo)
                att[h][b] = jnp.where(causal, _dot_nt(qb, kb), 0.0)
        o_inter = [_dot_nt(qh[h] * egh[h], st[h]) for h in heads]
        upd = [_dot_tn(vh[h], kh[h] * jnp.exp(g_last[h] - gh[h])) for h in heads]
        o_intra = [[_dot(att[h][b], vh[h][:hi]) for b, (lo, hi) in enumerate(blocks)] for h in heads]
        outs = []
        for h in heads:
            st_ref[h] = st[h] * jnp.exp(g_last[h]) + upd[h]
            o = o_inter[h] + jnp.concatenate(o_intra[h], axis=0)
            o = o * lax.rsqrt(jnp.mean(o * o, axis=-1, keepdims=True) + EPS) * gn_ref[...]
            outs.append(o * _silu(ogh[h]))
        for b in range(nbat):
            o_ref[b, rows, :] = jnp.concatenate(outs[b * NH_B:(b + 1) * NH_B], axis=1)
        return carry

    lax.fori_loop(0, tt // CHUNK, chunk, 0)


def _gla(q, k, v, og, gl, wg2p, bg2p, gnorm, bsz, seq):
    tt = min(TT_SEQ, seq)
    row = lambda t: (0, t, 0)
    const = lambda t: (0, 0)
    r3 = lambda a: a.reshape(bsz, seq, a.shape[-1])
    out = pl.pallas_call(
        _gla_body,
        out_shape=jax.ShapeDtypeStruct((bsz, seq, V_B), F32),
        grid=(seq // tt,),
        in_specs=[
            pl.BlockSpec((bsz, tt, KP_B), row),
            pl.BlockSpec((bsz, tt, KP_B), row),
            pl.BlockSpec((bsz, tt, V_B), row),
            pl.BlockSpec((bsz, tt, V_B), row),
            pl.BlockSpec((bsz, tt, LANES), row),
            pl.BlockSpec((LANES, KP_B), const),
            pl.BlockSpec((1, KP_B), const),
            pl.BlockSpec((1, DV_B), const),
        ],
        out_specs=pl.BlockSpec((bsz, tt, V_B), row),
        scratch_shapes=[pltpu.VMEM((bsz * NH_B, DV_B, DKP_B), F32)],
        compiler_params=_cparams(("arbitrary",)),
        name="gla",
    )(r3(q), r3(k), r3(v), r3(og), r3(gl), wg2p, bg2p, gnorm.reshape(1, DV_B))
    return out.reshape(bsz * seq, V_B)


def _unit_lower_solve(lows, rhss, ri, ci):
    n = range(len(lows))
    same = (ri // SUB) == (ci // SUB)
    eye = (ri == ci).astype(F32)
    dg = [jnp.where(same, lows[h], 0.0) for h in n]
    off = [lows[h] - dg[h] for h in n]
    tinv = [eye - dg[h] for h in n]
    pw = dg
    e = 2
    while e < SUB:
        pw = [_dot(pw[h], pw[h]) for h in n]
        tinv = [tinv[h] + _dot(tinv[h], pw[h]) for h in n]
        e *= 2
    nmat = [_dot(tinv[h], off[h]) for h in n]
    x = [_dot(tinv[h], rhss[h]) for h in n]
    sol = [[x[h][0:SUB]] for h in n]
    for blk in range(1, CHUNK // SUB):
        lo, hi = blk * SUB, (blk + 1) * SUB
        for h in n:
            prev = jnp.concatenate(sol[h], axis=0)
            sol[h].append(x[h][lo:hi] - _dot(nmat[h][lo:hi, :lo], prev))
    return [jnp.concatenate(sol[h], axis=0) for h in n]


def _delta_body(q_ref, k_ref, v_ref, z_ref, ba_ref, cq_ref, ck_ref, cv_ref, alog_ref, dtb_ref, dn_ref,
                o_ref, qbuf, kbuf, vbuf, qs, ks, vs, st_ref):
    t = pl.program_id(0)
    nbat, tt = q_ref.shape[0], q_ref.shape[1]

    @pl.when(t == 0)
    def _():
        st_ref[...] = jnp.zeros_like(st_ref)

    for b in range(nbat):
        qs[b] = _silu(_causal_conv(q_ref.at[b], cq_ref, qbuf.at[b], t == 0))
        ks[b] = _silu(_causal_conv(k_ref.at[b], ck_ref, kbuf.at[b], t == 0))
        vs[b] = _silu(_causal_conv(v_ref.at[b], cv_ref, vbuf.at[b], t == 0))

    ri = lax.broadcasted_iota(I32, (CHUNK, CHUNK), 0)
    ci = lax.broadcasted_iota(I32, (CHUNK, CHUNK), 1)
    tri_incl = (ri >= ci).astype(F32)
    incl = ri >= ci
    strict = ri > ci

    def chunk(c, carry):
        r0 = pl.multiple_of(c * CHUNK, CHUNK)
        rows = pl.ds(r0, CHUNK)
        heads = range(nbat * NH_C)
        qh, kh, vh, zh, beta, gcol, egcol, g_last, decay = [], [], [], [], [], [], [], [], []
        for b in range(nbat):
            ba = ba_ref[b, rows, :]
            beta_all = _sigmoid(ba)
            g_all = -jnp.exp(alog_ref[...]) * _softplus(ba + dtb_ref[...])
            gcum = _dot_hi(tri_incl, g_all)
            gcum_t = gcum.T
            egc = jnp.exp(gcum)
            q_all, k_all, v_all, z_all = qs[b, rows, :], ks[b, rows, :], vs[b, rows, :], z_ref[b, rows, :]
            for hh in range(NH_C):
                hs = slice(hh * DH_C, (hh + 1) * DH_C)
                q_, k_ = q_all[:, hs], k_all[:, hs]
                qh.append(q_ * lax.rsqrt(jnp.sum(q_ * q_, axis=-1, keepdims=True) + EPS) * (DH_C ** -0.5))
                kh.append(k_ * lax.rsqrt(jnp.sum(k_ * k_, axis=-1, keepdims=True) + EPS))
                vh.append(v_all[:, hs])
                zh.append(z_all[:, hs])
                beta.append(beta_all[:, hh:hh + 1])
                gcol.append(gcum[:, NH_C + hh:NH_C + hh + 1])
                egcol.append(egc[:, NH_C + hh:NH_C + hh + 1])
                g_last.append(gcum[CHUNK - 1:CHUNK, NH_C + hh:NH_C + hh + 1])
                grow = gcum_t[NH_C + hh:NH_C + hh + 1, :]
                decay.append(jnp.exp(jnp.where(incl, gcol[-1] - grow, -jnp.inf)))
        kbeta = [kh[h] * beta[h] for h in heads]
        kk = [_dot_nt(kbeta[h], kh[h]) for h in heads]
        qk = [_dot_nt(qh[h], kh[h]) for h in heads]
        low = [jnp.where(strict, kk[h] * decay[h], 0.0) for h in heads]
        rhs = [jnp.concatenate([vh[h] * beta[h], kbeta[h] * egcol[h]], axis=1) for h in heads]
        sol = _unit_lower_solve(low, rhs, ri, ci)
        st = [st_ref[h] for h in heads]
        o_inter = [_dot(qh[h] * egcol[h], st[h]) for h in heads]
        v_new = [sol[h][:, :DH_C] - _dot(sol[h][:, DH_C:], st[h]) for h in heads]
        o_intra = [_dot(qk[h] * decay[h], v_new[h]) for h in heads]
        upd = [_dot_tn(kh[h] * jnp.exp(g_last[h] - gcol[h]), v_new[h]) for h in heads]
        outs = []
        for h in heads:
            st_ref[h] = st[h] * jnp.exp(g_last[h]) + upd[h]
            o = o_inter[h] + o_intra[h]
            o = o * lax.rsqrt(jnp.mean(o * o, axis=-1, keepdims=True) + EPS) * dn_ref[...]
            outs.append(o * _silu(zh[h]))
        for b in range(nbat):
            o_ref[b, rows, :] = jnp.concatenate(outs[b * NH_C:(b + 1) * NH_C], axis=1)
        return carry

    lax.fori_loop(0, tt // CHUNK, chunk, 0)


def _delta(q, k, v, z, ba, cq, ck, cv, alog, dtb, dnorm, bsz, seq):
    tt = min(TT_SEQ, seq)
    row = lambda t: (0, t, 0)
    const = lambda t: (0, 0)
    big = pl.BlockSpec((bsz, tt, W_C), row)
    cw = pl.BlockSpec((CONV_W, W_C), const)
    vec = pl.BlockSpec((1, LANES), const)
    r3 = lambda a: a.reshape(bsz, seq, a.shape[-1])
    out = pl.pallas_call(
        _delta_body,
        out_shape=jax.ShapeDtypeStruct((bsz, seq, W_C), F32),
        grid=(seq // tt,),
        in_specs=[big, big, big, big, pl.BlockSpec((bsz, tt, LANES), row), cw, cw, cw, vec, vec, vec],
        out_specs=big,
        scratch_shapes=[pltpu.VMEM((bsz, tt + SUBLANES, W_C), F32)] * 3 + [pltpu.VMEM((bsz, tt, W_C), F32)] * 3
        + [pltpu.VMEM((bsz * NH_C, DH_C, DH_C), F32)],
        compiler_params=_cparams(("arbitrary",)),
        name="delta",
    )(r3(q), r3(k), r3(v), r3(z), r3(ba), cq, ck, cv, alog, dtb, dnorm)
    return out.reshape(bsz * seq, W_C)


def _outproj_router_body(n_parts, *refs):
    y_refs = refs[:n_parts]
    w_refs = refs[n_parts:2 * n_parts]
    (h_ref, mod_ref, g2_ref, wr_hi_ref, wr_lo_ref, br_ref,
     h1_ref, u2_ref, info_ref, cnt_ref, carry) = refs[2 * n_parts:]
    i = pl.program_id(0)
    tm = h_ref.shape[0]

    @pl.when(i == 0)
    def _():
        carry[...] = jnp.zeros_like(carry)

    m = None
    for y_ref, w_ref in zip(y_refs, w_refs):
        part = _dot(y_ref[...], w_ref[...])
        m = part if m is None else m + part
    h1 = h_ref[...] + mod_ref[2:3, :] * m
    h1_ref[...] = h1
    u2 = _rms_mod(h1, g2_ref[...], mod_ref[4:5, :], mod_ref[3:4, :])
    u2_ref[...] = u2

    lane = lax.broadcasted_iota(I32, (tm, LANES), 1)
    neg = -jnp.inf
    u_hi = u2.astype(BF16)
    u_lo = (u2 - u_hi.astype(F32)).astype(BF16)
    w_hi = wr_hi_ref[...]
    logits = (jnp.dot(u_hi, w_hi, preferred_element_type=F32)
              + jnp.dot(u_hi, wr_lo_ref[...], preferred_element_type=F32)
              + jnp.dot(u_lo, w_hi, preferred_element_type=F32)) + br_ref[...]
    lgp = jnp.where((lane >= N_EXP) & (lane < N_EXP + N_GROUPS), logits, neg)
    gmax = jnp.max(lgp, axis=-1, keepdims=True)
    pg_top = 1.0 / jnp.sum(jnp.exp(lgp - gmax), axis=-1, keepdims=True)
    g_idx = jnp.min(jnp.where(lgp == gmax, lane, LANES), axis=-1, keepdims=True) - N_EXP
    in_group = (lane // EPG == g_idx) & (lane < N_EXP)
    le = jnp.where(in_group, logits, neg)
    emax = jnp.max(le, axis=-1, keepdims=True)
    pe = jnp.exp(le - emax)
    pe = jnp.where(in_group, pe / jnp.sum(pe, axis=-1, keepdims=True), -1.0)
    p0 = jnp.max(pe, axis=-1, keepdims=True)
    e0 = jnp.min(jnp.where(pe == p0, lane, LANES), axis=-1, keepdims=True)
    pe1 = jnp.where(lane == e0, -1.0, pe)
    p1 = jnp.max(pe1, axis=-1, keepdims=True)
    e1 = jnp.min(jnp.where(pe1 == p1, lane, LANES), axis=-1, keepdims=True)
    den = p0 + p1
    w0 = pg_top * p0 / den
    w1 = pg_top * p1 / den
    hot = ((lane == e0) | (lane == e1)).astype(F32)
    ri = lax.broadcasted_iota(I32, (tm, tm), 0)
    ci = lax.broadcasted_iota(I32, (tm, tm), 1)
    before = _dot((ri > ci).astype(F32), hot) + carry[0:1, :]
    r0 = jnp.sum(jnp.where(lane == e0, before, 0.0), axis=-1, keepdims=True)
    r1 = jnp.sum(jnp.where(lane == e1, before, 0.0), axis=-1, keepdims=True)
    carry[...] = carry[...] + jnp.sum(hot, axis=0, keepdims=True)
    cnt_ref[...] = carry[...]
    info = jnp.where(lane == 0, e0.astype(F32), 0.0)
    info = jnp.where(lane == 1, e1.astype(F32), info)
    info = jnp.where(lane == 2, r0, info)
    info = jnp.where(lane == 3, r1, info)
    info = jnp.where(lane == 4, w0, info)
    info = jnp.where(lane == 5, w1, info)
    info_ref[...] = info


def _outproj_router(ys, ws, h, mod, g2, wr_hi, wr_lo, br, seq):
    t_rows = h.shape[0]
    tm = min(TM_PROJ, seq)
    per_b = seq // tm
    n = len(ys)
    row = lambda i: (i, 0)
    const = lambda i: (0, 0)
    in_specs = [pl.BlockSpec((tm, y.shape[1]), row) for y in ys]
    in_specs += [pl.BlockSpec(w.shape, const) for w in ws]
    in_specs += [
        pl.BlockSpec((tm, D), row),
        pl.BlockSpec((None, 6, D), lambda i: (i // per_b, 0, 0)),
        pl.BlockSpec((1, D), const),
        pl.BlockSpec((D, LANES), const),
        pl.BlockSpec((D, LANES), const),
        pl.BlockSpec((1, LANES), const),
    ]
    return pl.pallas_call(
        functools.partial(_outproj_router_body, n),
        out_shape=[
            jax.ShapeDtypeStruct((t_rows, D), F32),
            jax.ShapeDtypeStruct((t_rows, D), F32),
            jax.ShapeDtypeStruct((t_rows, LANES), F32),
            jax.ShapeDtypeStruct((SUBLANES, LANES), F32),
        ],
        grid=(t_rows // tm,),
        in_specs=in_specs,
        out_specs=[
            pl.BlockSpec((tm, D), row),
            pl.BlockSpec((tm, D), row),
            pl.BlockSpec((tm, LANES), row),
            pl.BlockSpec((SUBLANES, LANES), const),
        ],
        scratch_shapes=[pltpu.VMEM((SUBLANES, LANES), F32)],
        compiler_params=_cparams(("arbitrary",)),
        name="outproj_router",
    )(*ys, *ws, h, mod, g2.reshape(1, D), wr_hi, wr_lo, br)


def _row_copy(src, src_row, dst, dst_row, sem):
    return pltpu.make_async_copy(src.at[pl.ds(src_row, 1)], dst.at[pl.ds(dst_row, 1)], sem)


def _pos_body(info_ref, sp_ref, o_ref):
    info = info_ref[...]
    td = info.shape[0]
    lane = lax.broadcasted_iota(I32, (td, LANES), 1)
    sp = sp_ref[...]
    packed = jnp.zeros((td, LANES), F32)
    for k in range(TOP_K):
        e = info[:, k:k + 1].astype(I32)
        start = jnp.sum(jnp.where(lane == e, sp, 0.0), axis=-1, keepdims=True)
        packed = jnp.where(lane == k, start + info[:, TOP_K + k:TOP_K + k + 1], packed)
    o_ref[...] = packed.T[0:SUBLANES, :].astype(I32)


def _moe_pos(info, sp_row, td):
    t_rows = info.shape[0]
    nstep = t_rows // td
    return pl.pallas_call(
        _pos_body,
        out_shape=jax.ShapeDtypeStruct((nstep, SUBLANES, td), I32),
        grid=(nstep,),
        in_specs=[pl.BlockSpec((td, LANES), lambda i: (i, 0)), pl.BlockSpec((1, LANES), lambda i: (0, 0))],
        out_specs=pl.BlockSpec((None, SUBLANES, td), lambda i: (i, 0, 0)),
        compiler_params=_cparams(("arbitrary",)),
        name="moe_pos",
    )(info, sp_row)


def _dispatch_body(be_ref, nu_ref, pos_ref, u_ref, xb_out, zero_s, sems):
    i = pl.program_id(0)
    td = u_ref.shape[0]
    nb = be_ref.shape[0]

    @pl.when(i == 0)
    def _():
        zero_s[...] = jnp.zeros_like(zero_s)

        def needs_zero(b):
            return (b >= nu_ref[0] - 1) | (be_ref[jnp.minimum(b + 1, nb - 1)] != be_ref[b])

        def block_copy(b):
            return pltpu.make_async_copy(zero_s, xb_out.at[pl.ds(b * BM_MOE, BM_MOE)], sems.at[1])

        def zissue(b, c):
            @pl.when(needs_zero(b))
            def _():
                block_copy(b).start()
            return c

        def zdrain(b, c):
            @pl.when(needs_zero(b))
            def _():
                block_copy(b).wait()
            return c

        lax.fori_loop(0, nb, zissue, 0)
        lax.fori_loop(0, nb, zdrain, 0)

    def issue(j, c):
        for k in range(TOP_K):
            _row_copy(u_ref, j, xb_out, pos_ref[0, k, j], sems.at[0]).start()
        return c

    lax.fori_loop(0, td, issue, 0, unroll=DMA_UNROLL)

    def drain(j, c):
        for k in range(TOP_K):
            _row_copy(u_ref, j, xb_out, pos_ref[0, k, j], sems.at[0]).wait()
        return c

    lax.fori_loop(0, td, drain, 0, unroll=DMA_UNROLL)


def _dispatch(blk_e, n_used, pos, u2, n_rows, td):
    t_rows = u2.shape[0]
    nstep = t_rows // td
    return pl.pallas_call(
        _dispatch_body,
        out_shape=jax.ShapeDtypeStruct((n_rows, D), F32),
        grid_spec=pltpu.PrefetchScalarGridSpec(
            num_scalar_prefetch=2,
            grid=(nstep,),
            in_specs=[
                pl.BlockSpec((1, SUBLANES, td), lambda i, be, nu: (i, 0, 0), memory_space=pltpu.SMEM),
                pl.BlockSpec((td, D), lambda i, be, nu: (i, 0)),
            ],
            out_specs=pl.BlockSpec(memory_space=pl.ANY),
            scratch_shapes=[pltpu.VMEM((BM_MOE, D), F32), pltpu.SemaphoreType.DMA((2,))],
        ),
        compiler_params=_cparams(("arbitrary",)),
        name="moe_dispatch",
    )(blk_e, n_used, pos, u2)


def _ffn_body(be_ref, nu_ref, x_ref, w1_ref, w3_ref, w2_ref, o_ref, w13_s, w2_s):
    i = pl.program_id(0)
    prev = be_ref[jnp.maximum(i - 1, 0)]
    fresh = (i == 0) | (be_ref[i] != prev)

    @pl.when(fresh & (i < nu_ref[0]))
    def _():
        w13_s[:, :D_EXP] = w1_ref[...].astype(BF16)
        w13_s[:, D_EXP:] = w3_ref[...].astype(BF16)
        w2_s[...] = w2_ref[...].astype(BF16)

    @pl.when(i < nu_ref[0])
    def _():
        x = x_ref[...].astype(BF16)
        a = jnp.dot(x, w13_s[...], preferred_element_type=F32)
        hid = _silu(a[:, :D_EXP]) * a[:, D_EXP:]
        o_ref[...] = jnp.dot(hid.astype(BF16), w2_s[...], preferred_element_type=F32)

    @pl.when(i >= nu_ref[0])
    def _():
        o_ref[...] = jnp.zeros_like(o_ref)


def _ffn(blk_e, n_used, xb, w1, w3, w2, layer):
    n_rows = xb.shape[0]
    nb = n_rows // BM_MOE

    def last_used(i, nu):
        return jnp.maximum(jnp.minimum(i, nu[0] - 1), 0)

    def xmap(i, be, nu):
        return (last_used(i, nu), 0)

    def wmap(i, be, nu):
        return (layer, be[last_used(i, nu)], 0, 0)

    return pl.pallas_call(
        _ffn_body,
        out_shape=jax.ShapeDtypeStruct((n_rows, D), F32),
        grid_spec=pltpu.PrefetchScalarGridSpec(
            num_scalar_prefetch=2,
            grid=(nb,),
            in_specs=[
                pl.BlockSpec((BM_MOE, D), xmap),
                pl.BlockSpec((None, None, D, D_EXP), wmap),
                pl.BlockSpec((None, None, D, D_EXP), wmap),
                pl.BlockSpec((None, None, D_EXP, D), wmap),
            ],
            out_specs=pl.BlockSpec((BM_MOE, D), lambda i, be, nu: (i, 0)),
            scratch_shapes=[pltpu.VMEM((D, 2 * D_EXP), BF16), pltpu.VMEM((D_EXP, D), BF16)],
        ),
        compiler_params=_cparams(("arbitrary",)),
        name="moe_ffn",
    )(blk_e, n_used, xb, w1, w3, w2)


def _combine_body(final, pos_ref, nxt_ref, yb_hbm, h_ref, info_ref, mod_ref, fn_ref, o_ref, ybuf, sems):
    i = pl.program_id(0)
    n = pl.num_programs(0)
    td = h_ref.shape[0]
    slot = i % 2

    def gather(p_ref, s, wait):
        def body(j, c):
            for k in range(TOP_K):
                cp = _row_copy(yb_hbm, p_ref[0, k, j], ybuf.at[s, k], j, sems.at[s])
                cp.wait() if wait else cp.start()
            return c

        lax.fori_loop(0, td, body, 0, unroll=DMA_UNROLL)

    @pl.when(i == 0)
    def _():
        gather(pos_ref, slot, False)

    @pl.when(i + 1 < n)
    def _():
        gather(nxt_ref, 1 - slot, False)

    gather(pos_ref, slot, True)
    info = info_ref[...]
    y = info[:, 4:5] * ybuf[slot, 0] + info[:, 5:6] * ybuf[slot, 1]
    h2 = h_ref[...] + mod_ref[5:6, :] * y
    if final:
        h2 = h2 * lax.rsqrt(jnp.mean(h2 * h2, axis=-1, keepdims=True) + EPS) * fn_ref[...]
    o_ref[...] = h2


def _combine(pos, yb, h1, info, mod, fnorm, seq, td, final):
    t_rows = h1.shape[0]
    per_b = seq // td
    nstep = t_rows // td
    row = lambda i: (i, 0)
    return pl.pallas_call(
        functools.partial(_combine_body, final),
        out_shape=jax.ShapeDtypeStruct((t_rows, D), F32),
        grid=(nstep,),
        in_specs=[
            pl.BlockSpec((1, SUBLANES, td), lambda i: (i, 0, 0), memory_space=pltpu.SMEM),
            pl.BlockSpec((1, SUBLANES, td), lambda i: (jnp.minimum(i + 1, nstep - 1), 0, 0),
                         memory_space=pltpu.SMEM),
            pl.BlockSpec(memory_space=pl.ANY),
            pl.BlockSpec((td, D), row),
            pl.BlockSpec((td, LANES), row),
            pl.BlockSpec((None, 6, D), lambda i: (i // per_b, 0, 0)),
            pl.BlockSpec((1, D), lambda i: (0, 0)),
        ],
        out_specs=pl.BlockSpec((td, D), row),
        scratch_shapes=[pltpu.VMEM((2, TOP_K, td, D), F32), pltpu.SemaphoreType.DMA((2,))],
        compiler_params=_cparams(("arbitrary",)),
        name="moe_combine",
    )(pos, pos, yb, h1, info, mod, fnorm.reshape(1, D))


def _moe(u2, info, counts, h1, mod, w1, w3, w2, layer, fnorm, seq, final):
    t_rows = u2.shape[0]
    td = min(TD_MOE, seq)
    nb = (t_rows * TOP_K) // BM_MOE + N_EXP
    n_rows = nb * BM_MOE
    cnt = counts[0, :N_EXP].astype(I32)
    padded = (cnt + BM_MOE - 1) // BM_MOE * BM_MOE
    end_p = jnp.cumsum(padded)
    start_p = end_p - padded
    sp_row = jnp.zeros((1, LANES), F32).at[0, :N_EXP].set(start_p.astype(F32))
    blk_e = jnp.minimum(jnp.sum(jnp.arange(nb, dtype=I32)[:, None] * BM_MOE >= end_p[None, :], axis=1),
                        N_EXP - 1).astype(I32)
    n_used = (end_p[-1:] // BM_MOE).astype(I32)
    pos = _moe_pos(info, sp_row, td)
    xb = _dispatch(blk_e, n_used, pos, u2, n_rows, td)
    yb = _ffn(blk_e, n_used, xb, w1, w3, w2, layer)
    return _combine(pos, yb, h1, info, mod, fnorm, seq, td, final)


def _pad_cols(w, n):
    return jnp.pad(w, ((0, 0), (0, n - w.shape[1])))


def _pad_heads(w):
    w = w.reshape(w.shape[:-1] + (NH_B, DK_B))
    w = jnp.pad(w, [(0, 0)] * (w.ndim - 1) + [(0, DKP_B - DK_B)])
    return w.reshape(w.shape[:-2] + (KP_B,))


def _block_diag(w):
    nh, hw, _ = w.shape
    eye = jnp.eye(nh, dtype=w.dtype)
    return (eye[:, None, :, None] * w[:, :, None, :]).reshape(nh * hw, nh * hw)


def _router_weights(w_grp, b_grp, w_rt, b_rt):
    w = _pad_cols(jnp.concatenate([w_rt, w_grp], axis=1), LANES)
    b = _pad_cols(jnp.concatenate([b_rt, b_grp]).reshape(1, -1), LANES)
    w_hi = w.astype(BF16)
    w_lo = (w - w_hi.astype(F32)).astype(BF16)
    return w_hi, w_lo, b


def kernel(x, c, norm1, norm2, w_ada, b_ada, w_in_ab, conv_a_w, conv_a_b, rg_wa, rg_ba, rg_wx, rg_bx, rg_lam, gla_wg2, gla_bg2, gla_norm, w_out_ab, w_in_c, conv_c_w, dn_a_log, dn_dt_bias, dn_norm, w_out_c, moe_w_grp, moe_b_grp, moe_w_rt, moe_b_rt, moe_w1, moe_w3, moe_w2, final_norm):
    bsz, seq, _ = x.shape
    depth = w_ada.shape[0]
    t_rows = bsz * seq
    mod_all = _ada_mod(c, w_ada, b_ada)
    h = x.reshape(t_rows, D)
    for layer in range(depth):
        mod = mod_all[layer]
        j = layer // 2
        if layer % 2 == 0:
            wi = w_in_ab[j]
            o_q, o_k, o_v, o_gl = 2 * W_A, 2 * W_A + K_B, 2 * W_A + 2 * K_B, 2 * W_A + 2 * K_B + 2 * V_B
            w_in = jnp.concatenate([
                wi[:, :o_q], _pad_heads(wi[:, o_q:o_k]), _pad_heads(wi[:, o_k:o_v]), wi[:, o_v:o_gl],
                _pad_cols(wi[:, o_gl:], LANES)], axis=1).astype(BF16)
            xa, ga, q, k, v, og, gl = _inproj(h, mod, norm1[layer], w_in,
                                              (W_A, W_A, KP_B, KP_B, V_B, V_B, LANES), seq)
            wg_bd = jnp.concatenate([_block_diag(rg_wa[j]), _block_diag(rg_wx[j])], axis=1).astype(BF16)
            bg = jnp.concatenate([rg_ba[j], rg_bx[j]])
            ya = _rglru(xa, ga, conv_a_w[j], conv_a_b[j], wg_bd, bg, rg_lam[j], bsz, seq)
            wg2p = jnp.pad(_pad_heads(gla_wg2[j]), ((0, LANES - R_GATE), (0, 0)))
            bg2p = _pad_heads(gla_bg2[j].reshape(1, K_B))
            ob = _gla(q, k, v, og, gl, wg2p, bg2p, gla_norm[j], bsz, seq)
            ys = (ya, ob)
            ws = (w_out_ab[j][:W_A].astype(BF16), w_out_ab[j][W_A:].astype(BF16))
        else:
            p_c = w_in_c.shape[2]
            w_in = _pad_cols(w_in_c[j], p_c - 2 * NH_C + LANES).astype(BF16)
            q, k, v, z, ba = _inproj(h, mod, norm1[layer], w_in, (W_C, W_C, W_C, W_C, LANES), seq)
            cw = conv_c_w[j]
            lanes_c = jnp.zeros((1, LANES), F32)
            alog = lanes_c.at[0, NH_C:2 * NH_C].set(dn_a_log[j])
            dtb = lanes_c.at[0, NH_C:2 * NH_C].set(dn_dt_bias[j])
            o = _delta(q, k, v, z, ba, cw[:, :W_C], cw[:, W_C:2 * W_C], cw[:, 2 * W_C:], alog, dtb,
                       dn_norm[j].reshape(1, DH_C), bsz, seq)
            ys = (o,)
            ws = (w_out_c[j].astype(BF16),)
        rw = _router_weights(moe_w_grp[layer], moe_b_grp[layer], moe_w_rt[layer], moe_b_rt[layer])
        h1, u2, info, counts = _outproj_router(ys, ws, h, mod, norm2[layer], *rw, seq)
        h = _moe(u2, info, counts, h1, mod, moe_w1, moe_w3, moe_w2, layer, final_norm, seq,
                 final=layer == depth - 1)
    return h.reshape(bsz, seq, D)
```

```python
import functools

import jax
import jax.numpy as jnp
from jax import lax
from jax.experimental import pallas as pl
from jax.experimental.pallas import tpu as pltpu

F32 = jnp.float32
BF16 = jnp.bfloat16
I32 = jnp.int32
HI = lax.Precision.HIGHEST

D = 1024
CHUNK = 64
SUB = 16
CONV_W = 4
EPS = 1e-6
LANES = 128
SUBLANES = 8

W_A = D // 2
NH_A = 8
HW_A = W_A // NH_A
RG_C = 8.0
NH_B = 4
DK_B = 64
DV_B = 128
K_B = NH_B * DK_B
V_B = NH_B * DV_B
DKP_B = LANES
KP_B = NH_B * DKP_B
R_GATE = 16
GATE_NORM = 16.0

NH_C = 8
DH_C = 128
W_C = NH_C * DH_C

N_GROUPS = 4
EPG = 8
N_EXP = N_GROUPS * EPG
TOP_K = 2
D_EXP = 512

VMEM_LIMIT = 56 * 1024 * 1024

TM_PROJ = 512
TT_SEQ = 256
BM_MOE = 256
TD_MOE = 256
TP_MOE = 2048
DMA_UNROLL = 8


def _cparams(sem):
    return pltpu.CompilerParams(dimension_semantics=sem, vmem_limit_bytes=VMEM_LIMIT)


def _softplus(x):
    return jnp.maximum(x, 0.0) + jnp.log1p(jnp.exp(-jnp.abs(x)))


def _sigmoid(x):
    return jax.nn.sigmoid(x)


def _silu(x):
    return x * jax.nn.sigmoid(x)


def _rms_mod(h, g, sc, sh):
    y = h * lax.rsqrt(jnp.mean(h * h, axis=-1, keepdims=True) + EPS)
    return y * g * (1.0 + sc) + sh


def _dot(a, b):
    return jnp.dot(a.astype(BF16), b.astype(BF16), preferred_element_type=F32)


def _dot_nt(a, b):
    return lax.dot_general(a.astype(BF16), b.astype(BF16), (((1,), (1,)), ((), ())),
                           preferred_element_type=F32)


def _dot_tn(a, b):
    return lax.dot_general(a.astype(BF16), b.astype(BF16), (((0,), (0,)), ((), ())),
                           preferred_element_type=F32)


def _dot_hi(a, b):
    return jnp.dot(a, b, precision=HI, preferred_element_type=F32)


def _ada_body(c_ref, w_ref, b_ref, o_ref):
    o_ref[...] = _dot_hi(_silu(c_ref[...]), w_ref[...]) + b_ref[...]


def _ada_mod(c, w_ada, b_ada):
    depth, _, n6 = w_ada.shape
    bsz = c.shape[0]
    nt = 1536
    c8 = jnp.zeros((SUBLANES, D), F32).at[:bsz].set(c)
    out = pl.pallas_call(
        _ada_body,
        out_shape=jax.ShapeDtypeStruct((depth, SUBLANES, n6), F32),
        grid=(depth, n6 // nt),
        in_specs=[
            pl.BlockSpec((SUBLANES, D), lambda l, j: (0, 0)),
            pl.BlockSpec((None, D, nt), lambda l, j: (l, 0, j)),
            pl.BlockSpec((None, 1, nt), lambda l, j: (l, 0, j)),
        ],
        out_specs=pl.BlockSpec((None, SUBLANES, nt), lambda l, j: (l, 0, j)),
        compiler_params=_cparams(("arbitrary", "arbitrary")),
        name="ada_mod",
    )(c8, w_ada, b_ada.reshape(depth, 1, n6))
    return out[:, :bsz].reshape(depth, bsz, 6, D)


def _inproj_body(widths, h_ref, mod_ref, g_ref, w_ref, *out_refs):
    u = _rms_mod(h_ref[...], g_ref[...], mod_ref[1:2, :], mod_ref[0:1, :]).astype(BF16)
    off = 0
    for o_ref, wd in zip(out_refs, widths):
        o_ref[...] = jnp.dot(u, w_ref[:, off:off + wd], preferred_element_type=F32)
        off += wd


def _inproj(h, mod, g, w_bf16, widths, seq):
    t_rows = h.shape[0]
    tm = min(TM_PROJ, seq)
    per_b = seq // tm
    p = w_bf16.shape[1]
    assert sum(widths) == p
    return pl.pallas_call(
        functools.partial(_inproj_body, widths),
        out_shape=[jax.ShapeDtypeStruct((t_rows, wd), F32) for wd in widths],
        grid=(t_rows // tm,),
        in_specs=[
            pl.BlockSpec((tm, D), lambda i: (i, 0)),
            pl.BlockSpec((None, 6, D), lambda i: (i // per_b, 0, 0)),
            pl.BlockSpec((1, D), lambda i: (0, 0)),
            pl.BlockSpec((D, p), lambda i: (0, 0)),
        ],
        out_specs=[pl.BlockSpec((tm, wd), lambda i: (i, 0)) for wd in widths],
        compiler_params=_cparams(("arbitrary",)),
        name="inproj",
    )(h, mod, g.reshape(1, D), w_bf16)


def _causal_conv(x_ref, cw_ref, xbuf, first):
    tt = x_ref.shape[0]

    @pl.when(first)
    def _():
        xbuf[0:SUBLANES, :] = jnp.zeros((SUBLANES, xbuf.shape[1]), F32)

    xbuf[SUBLANES:SUBLANES + tt, :] = x_ref[...]
    acc = None
    for w in range(CONV_W):
        lo = SUBLANES - (CONV_W - 1) + w
        term = cw_ref[w:w + 1, :] * xbuf[lo:lo + tt, :]
        acc = term if acc is None else acc + term
    xbuf[0:SUBLANES, :] = xbuf[tt:tt + SUBLANES, :]
    return acc


def _rglru_body(xa_ref, ga_ref, cw_ref, cb_ref, wg_ref, bg_ref, lam_ref, o_ref, xbuf, hc):
    t = pl.program_id(1)
    tt = xa_ref.shape[0]

    @pl.when(t == 0)
    def _():
        hc[...] = jnp.zeros_like(hc)

    xc = _causal_conv(xa_ref, cw_ref, xbuf, t == 0) + cb_ref[...]
    gates = _dot(xc, wg_ref[...]) + bg_ref[...]
    r = _sigmoid(gates[:, :W_A])
    i = _sigmoid(gates[:, W_A:])
    log_a = -RG_C * r * _softplus(-lam_ref[...])
    a = jnp.exp(log_a)
    b = jnp.sqrt(1.0 - a * a) * (i * xc)
    row = lax.broadcasted_iota(I32, (tt, W_A), 0)
    s = 1
    while s < tt:
        a_sh = pltpu.roll(a, s, 0)
        b_sh = pltpu.roll(b, s, 0)
        valid = row >= s
        b = jnp.where(valid, a * b_sh + b, b)
        a = jnp.where(valid, a * a_sh, a)
        s *= 2
    h = b + a * hc[0:1, :]
    hc[...] = jnp.broadcast_to(h[tt - 1:tt, :], hc.shape)
    o_ref[...] = jax.nn.gelu(ga_ref[...], approximate=True) * h


def _rglru(xa, ga, conv_w, conv_b, wg_bd, bg, lam, bsz, seq):
    tt = min(TT_SEQ, seq)
    per_b = seq // tt
    row = lambda b, t: (b * per_b + t, 0)
    const = lambda b, t: (0, 0)
    return pl.pallas_call(
        _rglru_body,
        out_shape=jax.ShapeDtypeStruct(xa.shape, F32),
        grid=(bsz, per_b),
        in_specs=[
            pl.BlockSpec((tt, W_A), row),
            pl.BlockSpec((tt, W_A), row),
            pl.BlockSpec((CONV_W, W_A), const),
            pl.BlockSpec((1, W_A), const),
            pl.BlockSpec((W_A, 2 * W_A), const),
            pl.BlockSpec((1, 2 * W_A), const),
            pl.BlockSpec((1, W_A), const),
        ],
        out_specs=pl.BlockSpec((tt, W_A), row),
        scratch_shapes=[pltpu.VMEM((tt + SUBLANES, W_A), F32), pltpu.VMEM((SUBLANES, W_A), F32)],
        compiler_params=_cparams(("arbitrary", "arbitrary")),
        name="rglru",
    )(xa, ga, conv_w, conv_b.reshape(1, W_A), wg_bd, bg.reshape(1, 2 * W_A), lam.reshape(1, W_A))


def _gla_body(q_ref, k_ref, v_ref, og_ref, gl_ref, wg_ref, bg_ref, gn_ref, o_ref, st_ref):
    t = pl.program_id(0)
    nbat, tt = q_ref.shape[0], q_ref.shape[1]

    @pl.when(t == 0)
    def _():
        st_ref[...] = jnp.zeros_like(st_ref)

    ri = lax.broadcasted_iota(I32, (CHUNK, CHUNK), 0)
    ci = lax.broadcasted_iota(I32, (CHUNK, CHUNK), 1)
    tri_incl = (ri >= ci).astype(F32)

    def chunk(c, carry):
        r0 = pl.multiple_of(c * CHUNK, CHUNK)
        rows = pl.ds(r0, CHUNK)
        heads = range(nbat * NH_B)
        blocks = [(b * SUB, (b + 1) * SUB) for b in range(CHUNK // SUB)]
        qh, kh, gh, egh, vh, ogh = [], [], [], [], [], []
        for b in range(nbat):
            x = _dot_hi(gl_ref[b, rows, :], wg_ref[...]) + bg_ref[...]
            lg = -_softplus(-x) * (1.0 / GATE_NORM)
            g = _dot_hi(tri_incl, lg)
            eg = jnp.exp(g)
            q_all = q_ref[b, rows, :] * (DK_B ** -0.5)
            k_all = k_ref[b, rows, :]
            v_all = v_ref[b, rows, :]
            og_all = og_ref[b, rows, :]
            for hh in range(NH_B):
                ks_ = slice(hh * DKP_B, (hh + 1) * DKP_B)
                vs_ = slice(hh * DV_B, (hh + 1) * DV_B)
                qh.append(q_all[:, ks_])
                kh.append(k_all[:, ks_])
                gh.append(g[:, ks_])
                egh.append(eg[:, ks_])
                vh.append(v_all[:, vs_])
                ogh.append(og_all[:, vs_])
        st = [st_ref[h] for h in heads]
        g_last = [gh[h][CHUNK - 1:CHUNK, :] for h in heads]
        att = [[None] * len(blocks) for _ in heads]
        for h in heads:
            for b, (lo, hi) in enumerate(blocks):
                ref_g = gh[h][lo:lo + 1, :]
                qb = qh[h][lo:hi] * jnp.exp(gh[h][lo:hi] - ref_g)
                kb = kh[h][:hi] * jnp.exp(ref_g - gh[h][:hi])
                causal = (lax.broadcasted_iota(I32, (SUB, hi), 1)
                          <= lax.broadcasted_iota(I32, (SUB, hi), 0) + lo)
                att[h][b] = jnp.where(causal, _dot_nt(qb, kb), 0.0)
        o_inter = [_dot_nt(qh[h] * egh[h], st[h]) for h in heads]
        upd = [_dot_tn(vh[h], kh[h] * jnp.exp(g_last[h] - gh[h])) for h in heads]
        o_intra = [[_dot(att[h][b], vh[h][:hi]) for b, (lo, hi) in enumerate(blocks)] for h in heads]
        outs = []
        for h in heads:
            st_ref[h] = st[h] * jnp.exp(g_last[h]) + upd[h]
            o = o_inter[h] + jnp.concatenate(o_intra[h], axis=0)
            o = o * lax.rsqrt(jnp.mean(o * o, axis=-1, keepdims=True) + EPS) * gn_ref[...]
            outs.append(o * _silu(ogh[h]))
        for b in range(nbat):
            o_ref[b, rows, :] = jnp.concatenate(outs[b * NH_B:(b + 1) * NH_B], axis=1)
        return carry

    lax.fori_loop(0, tt // CHUNK, chunk, 0)


def _gla(q, k, v, og, gl, wg2p, bg2p, gnorm, bsz, seq):
    tt = min(TT_SEQ, seq)
    row = lambda t: (0, t, 0)
    const = lambda t: (0, 0)
    r3 = lambda a: a.reshape(bsz, seq, a.shape[-1])
    out = pl.pallas_call(
        _gla_body,
        out_shape=jax.ShapeDtypeStruct((bsz, seq, V_B), F32),
        grid=(seq // tt,),
        in_specs=[
            pl.BlockSpec((bsz, tt, KP_B), row),
            pl.BlockSpec((bsz, tt, KP_B), row),
            pl.BlockSpec((bsz, tt, V_B), row),
            pl.BlockSpec((bsz, tt, V_B), row),
            pl.BlockSpec((bsz, tt, LANES), row),
            pl.BlockSpec((LANES, KP_B), const),
            pl.BlockSpec((1, KP_B), const),
            pl.BlockSpec((1, DV_B), const),
        ],
        out_specs=pl.BlockSpec((bsz, tt, V_B), row),
        scratch_shapes=[pltpu.VMEM((bsz * NH_B, DV_B, DKP_B), F32)],
        compiler_params=_cparams(("arbitrary",)),
        name="gla",
    )(r3(q), r3(k), r3(v), r3(og), r3(gl), wg2p, bg2p, gnorm.reshape(1, DV_B))
    return out.reshape(bsz * seq, V_B)


def _unit_lower_solve(lows, rhss, ri, ci):
    n = range(len(lows))
    same = (ri // SUB) == (ci // SUB)
    eye = (ri == ci).astype(F32)
    dg = [jnp.where(same, lows[h], 0.0) for h in n]
    off = [lows[h] - dg[h] for h in n]
    tinv = [eye - dg[h] for h in n]
    pw = dg
    e = 2
    while e < SUB:
        pw = [_dot(pw[h], pw[h]) for h in n]
        tinv = [tinv[h] + _dot(tinv[h], pw[h]) for h in n]
        e *= 2
    nmat = [_dot(tinv[h], off[h]) for h in n]
    x = [_dot(tinv[h], rhss[h]) for h in n]
    sol = [[x[h][0:SUB]] for h in n]
    for blk in range(1, CHUNK // SUB):
        lo, hi = blk * SUB, (blk + 1) * SUB
        for h in n:
            prev = jnp.concatenate(sol[h], axis=0)
            sol[h].append(x[h][lo:hi] - _dot(nmat[h][lo:hi, :lo], prev))
    return [jnp.concatenate(sol[h], axis=0) for h in n]


def _delta_body(q_ref, k_ref, v_ref, z_ref, ba_ref, cq_ref, ck_ref, cv_ref, alog_ref, dtb_ref, dn_ref,
                o_ref, qbuf, kbuf, vbuf, qs, ks, vs, st_ref):
    t = pl.program_id(0)
    nbat, tt = q_ref.shape[0], q_ref.shape[1]

    @pl.when(t == 0)
    def _():
        st_ref[...] = jnp.zeros_like(st_ref)

    for b in range(nbat):
        qs[b] = _silu(_causal_conv(q_ref.at[b], cq_ref, qbuf.at[b], t == 0))
        ks[b] = _silu(_causal_conv(k_ref.at[b], ck_ref, kbuf.at[b], t == 0))
        vs[b] = _silu(_causal_conv(v_ref.at[b], cv_ref, vbuf.at[b], t == 0))

    ri = lax.broadcasted_iota(I32, (CHUNK, CHUNK), 0)
    ci = lax.broadcasted_iota(I32, (CHUNK, CHUNK), 1)
    tri_incl = (ri >= ci).astype(F32)
    incl = ri >= ci
    strict = ri > ci

    def chunk(c, carry):
        r0 = pl.multiple_of(c * CHUNK, CHUNK)
        rows = pl.ds(r0, CHUNK)
        heads = range(nbat * NH_C)
        qh, kh, vh, zh, beta, gcol, egcol, g_last, decay = [], [], [], [], [], [], [], [], []
        for b in range(nbat):
            ba = ba_ref[b, rows, :]
            beta_all = _sigmoid(ba)
            g_all = -jnp.exp(alog_ref[...]) * _softplus(ba + dtb_ref[...])
            gcum = _dot_hi(tri_incl, g_all)
            gcum_t = gcum.T
            egc = jnp.exp(gcum)
            q_all, k_all, v_all, z_all = qs[b, rows, :], ks[b, rows, :], vs[b, rows, :], z_ref[b, rows, :]
            for hh in range(NH_C):
                hs = slice(hh * DH_C, (hh + 1) * DH_C)
                q_, k_ = q_all[:, hs], k_all[:, hs]
                qh.append(q_ * lax.rsqrt(jnp.sum(q_ * q_, axis=-1, keepdims=True) + EPS) * (DH_C ** -0.5))
                kh.append(k_ * lax.rsqrt(jnp.sum(k_ * k_, axis=-1, keepdims=True) + EPS))
                vh.append(v_all[:, hs])
                zh.append(z_all[:, hs])
                beta.append(beta_all[:, hh:hh + 1])
                gcol.append(gcum[:, NH_C + hh:NH_C + hh + 1])
                egcol.append(egc[:, NH_C + hh:NH_C + hh + 1])
                g_last.append(gcum[CHUNK - 1:CHUNK, NH_C + hh:NH_C + hh + 1])
                grow = gcum_t[NH_C + hh:NH_C + hh + 1, :]
                decay.append(jnp.exp(jnp.where(incl, gcol[-1] - grow, -jnp.inf)))
        kbeta = [kh[h] * beta[h] for h in heads]
        kq = [_dot_nt(jnp.concatenate([kbeta[h], qh[h]], axis=0), kh[h]) for h in heads]
        low = [jnp.where(strict, kq[h][:CHUNK] * decay[h], 0.0) for h in heads]
        rhs = [jnp.concatenate([vh[h] * beta[h], kbeta[h] * egcol[h]], axis=1) for h in heads]
        sol = _unit_lower_solve(low, rhs, ri, ci)
        st = [st_ref[h] for h in heads]
        ws = [_dot(jnp.concatenate([sol[h][:, DH_C:], qh[h] * egcol[h]], axis=0), st[h]) for h in heads]
        o_inter = [ws[h][CHUNK:] for h in heads]
        v_new = [sol[h][:, :DH_C] - ws[h][:CHUNK] for h in heads]
        o_intra = [_dot(kq[h][CHUNK:] * decay[h], v_new[h]) for h in heads]
        upd = [_dot_tn(kh[h] * jnp.exp(g_last[h] - gcol[h]), v_new[h]) for h in heads]
        outs = []
        for h in heads:
            st_ref[h] = st[h] * jnp.exp(g_last[h]) + upd[h]
            o = o_inter[h] + o_intra[h]
            o = o * lax.rsqrt(jnp.mean(o * o, axis=-1, keepdims=True) + EPS) * dn_ref[...]
            outs.append(o * _silu(zh[h]))
        for b in range(nbat):
            o_ref[b, rows, :] = jnp.concatenate(outs[b * NH_C:(b + 1) * NH_C], axis=1)
        return carry

    lax.fori_loop(0, tt // CHUNK, chunk, 0)


def _delta(q, k, v, z, ba, cq, ck, cv, alog, dtb, dnorm, bsz, seq):
    tt = min(TT_SEQ, seq)
    row = lambda t: (0, t, 0)
    const = lambda t: (0, 0)
    big = pl.BlockSpec((bsz, tt, W_C), row)
    cw = pl.BlockSpec((CONV_W, W_C), const)
    vec = pl.BlockSpec((1, LANES), const)
    r3 = lambda a: a.reshape(bsz, seq, a.shape[-1])
    out = pl.pallas_call(
        _delta_body,
        out_shape=jax.ShapeDtypeStruct((bsz, seq, W_C), F32),
        grid=(seq // tt,),
        in_specs=[big, big, big, big, pl.BlockSpec((bsz, tt, LANES), row), cw, cw, cw, vec, vec, vec],
        out_specs=big,
        scratch_shapes=[pltpu.VMEM((bsz, tt + SUBLANES, W_C), F32)] * 3 + [pltpu.VMEM((bsz, tt, W_C), F32)] * 3
        + [pltpu.VMEM((bsz * NH_C, DH_C, DH_C), F32)],
        compiler_params=_cparams(("arbitrary",)),
        name="delta",
    )(r3(q), r3(k), r3(v), r3(z), r3(ba), cq, ck, cv, alog, dtb, dnorm)
    return out.reshape(bsz * seq, W_C)


def _outproj_router_body(n_parts, *refs):
    y_refs = refs[:n_parts]
    w_refs = refs[n_parts:2 * n_parts]
    (h_ref, mod_ref, g2_ref, wr_hi_ref, wr_lo_ref, br_ref,
     h1_ref, u2_ref, info_ref, cnt_ref, carry) = refs[2 * n_parts:]
    i = pl.program_id(0)
    tm = h_ref.shape[0]

    @pl.when(i == 0)
    def _():
        carry[...] = jnp.zeros_like(carry)

    m = None
    for y_ref, w_ref in zip(y_refs, w_refs):
        part = _dot(y_ref[...], w_ref[...])
        m = part if m is None else m + part
    h1 = h_ref[...] + mod_ref[2:3, :] * m
    h1_ref[...] = h1
    u2 = _rms_mod(h1, g2_ref[...], mod_ref[4:5, :], mod_ref[3:4, :])
    _to_row_tiles(u2_ref, u2)

    lane = lax.broadcasted_iota(I32, (tm, LANES), 1)
    neg = -jnp.inf
    u_hi = u2.astype(BF16)
    u_lo = (u2 - u_hi.astype(F32)).astype(BF16)
    w_hi = wr_hi_ref[...]
    logits = (jnp.dot(u_hi, w_hi, preferred_element_type=F32)
              + jnp.dot(u_hi, wr_lo_ref[...], preferred_element_type=F32)
              + jnp.dot(u_lo, w_hi, preferred_element_type=F32)) + br_ref[...]
    lgp = jnp.where((lane >= N_EXP) & (lane < N_EXP + N_GROUPS), logits, neg)
    gmax = jnp.max(lgp, axis=-1, keepdims=True)
    pg_top = 1.0 / jnp.sum(jnp.exp(lgp - gmax), axis=-1, keepdims=True)
    g_idx = jnp.min(jnp.where(lgp == gmax, lane, LANES), axis=-1, keepdims=True) - N_EXP
    in_group = (lane // EPG == g_idx) & (lane < N_EXP)
    le = jnp.where(in_group, logits, neg)
    emax = jnp.max(le, axis=-1, keepdims=True)
    pe = jnp.exp(le - emax)
    pe = jnp.where(in_group, pe / jnp.sum(pe, axis=-1, keepdims=True), -1.0)
    p0 = jnp.max(pe, axis=-1, keepdims=True)
    e0 = jnp.min(jnp.where(pe == p0, lane, LANES), axis=-1, keepdims=True)
    pe1 = jnp.where(lane == e0, -1.0, pe)
    p1 = jnp.max(pe1, axis=-1, keepdims=True)
    e1 = jnp.min(jnp.where(pe1 == p1, lane, LANES), axis=-1, keepdims=True)
    den = p0 + p1
    w0 = pg_top * p0 / den
    w1 = pg_top * p1 / den
    hot = ((lane == e0) | (lane == e1)).astype(F32)
    ri = lax.broadcasted_iota(I32, (tm, tm), 0)
    ci = lax.broadcasted_iota(I32, (tm, tm), 1)
    before = _dot((ri > ci).astype(F32), hot) + carry[0:1, :]
    r0 = jnp.sum(jnp.where(lane == e0, before, 0.0), axis=-1, keepdims=True)
    r1 = jnp.sum(jnp.where(lane == e1, before, 0.0), axis=-1, keepdims=True)
    carry[...] = carry[...] + jnp.sum(hot, axis=0, keepdims=True)
    cnt_ref[...] = carry[...]
    info = jnp.where(lane == 0, e0.astype(F32), 0.0)
    info = jnp.where(lane == 1, e1.astype(F32), info)
    info = jnp.where(lane == 2, r0, info)
    info = jnp.where(lane == 3, r1, info)
    info = jnp.where(lane == 4, w0, info)
    info = jnp.where(lane == 5, w1, info)
    info_ref[...] = info


def _outproj_router(ys, ws, h, mod, g2, wr_hi, wr_lo, br, seq):
    t_rows = h.shape[0]
    tm = min(TM_PROJ, seq)
    per_b = seq // tm
    n = len(ys)
    row = lambda i: (i, 0)
    const = lambda i: (0, 0)
    in_specs = [pl.BlockSpec((tm, y.shape[1]), row) for y in ys]
    in_specs += [pl.BlockSpec(w.shape, const) for w in ws]
    in_specs += [
        pl.BlockSpec((tm, D), row),
        pl.BlockSpec((None, 6, D), lambda i: (i // per_b, 0, 0)),
        pl.BlockSpec((1, D), const),
        pl.BlockSpec((D, LANES), const),
        pl.BlockSpec((D, LANES), const),
        pl.BlockSpec((1, LANES), const),
    ]
    return pl.pallas_call(
        functools.partial(_outproj_router_body, n),
        out_shape=[
            jax.ShapeDtypeStruct((t_rows, D), F32),
            jax.ShapeDtypeStruct((t_rows * RT, LANES), F32),
            jax.ShapeDtypeStruct((t_rows, LANES), F32),
            jax.ShapeDtypeStruct((SUBLANES, LANES), F32),
        ],
        grid=(t_rows // tm,),
        in_specs=in_specs,
        out_specs=[
            pl.BlockSpec((tm, D), row),
            pl.BlockSpec((tm * RT, LANES), row),
            pl.BlockSpec((tm, LANES), row),
            pl.BlockSpec((SUBLANES, LANES), const),
        ],
        scratch_shapes=[pltpu.VMEM((SUBLANES, LANES), F32)],
        compiler_params=_cparams(("arbitrary",)),
        name="outproj_router",
    )(*ys, *ws, h, mod, g2.reshape(1, D), wr_hi, wr_lo, br)


RT = D // LANES
assert RT == SUBLANES


def _to_row_tiles(ref, x):
    n = x.shape[0]
    for c in range(RT):
        ref[pl.ds(c, n, stride=RT), :] = x[:, c * LANES:(c + 1) * LANES]


def _from_row_tiles(ref, n):
    return jnp.concatenate([ref[pl.ds(c, n, stride=RT), :] for c in range(RT)], axis=1)


def _row_copy(src, src_row, dst, dst_row, sem):
    return pltpu.make_async_copy(src.at[pl.ds(pl.multiple_of(src_row * RT, RT), RT)],
                                 dst.at[pl.ds(pl.multiple_of(dst_row * RT, RT), RT)], sem)


def _pos_body(info_ref, sp_ref, o_ref):
    info = info_ref[...]
    td = info.shape[0]
    lane = lax.broadcasted_iota(I32, (td, LANES), 1)
    sp = sp_ref[...]
    packed = jnp.zeros((td, LANES), F32)
    for k in range(TOP_K):
        e = info[:, k:k + 1].astype(I32)
        start = jnp.sum(jnp.where(lane == e, sp, 0.0), axis=-1, keepdims=True)
        packed = jnp.where(lane == k, start + info[:, TOP_K + k:TOP_K + k + 1], packed)
    pick = (lax.broadcasted_iota(I32, (SUBLANES, LANES), 0)
            == lax.broadcasted_iota(I32, (SUBLANES, LANES), 1)).astype(F32)
    o_ref[...] = lax.dot_general(pick, packed, (((1,), (1,)), ((), ())), precision=HI,
                                 preferred_element_type=F32).astype(I32)


def _moe_pos(info, sp_row):
    t_rows = info.shape[0]
    tp = min(TP_MOE, t_rows)
    return pl.pallas_call(
        _pos_body,
        out_shape=jax.ShapeDtypeStruct((SUBLANES, t_rows), I32),
        grid=(t_rows // tp,),
        in_specs=[pl.BlockSpec((tp, LANES), lambda i: (i, 0)), pl.BlockSpec((1, LANES), lambda i: (0, 0))],
        out_specs=pl.BlockSpec((SUBLANES, tp), lambda i: (0, i)),
        compiler_params=_cparams(("arbitrary",)),
        name="moe_pos",
    )(info, sp_row)


def _dispatch_body(be_ref, nu_ref, pos_ref, u_ref, xb_out, zero_s, sems):
    i = pl.program_id(0)
    td = u_ref.shape[0] // RT
    nb = be_ref.shape[0]

    @pl.when(i == 0)
    def _():
        zero_s[...] = jnp.zeros_like(zero_s)

        def needs_zero(b):
            return (b >= nu_ref[0] - 1) | (be_ref[jnp.minimum(b + 1, nb - 1)] != be_ref[b])

        def block_copy(b):
            return pltpu.make_async_copy(zero_s, xb_out.at[pl.ds(b * (BM_MOE * RT), BM_MOE * RT)], sems.at[1])

        def zissue(b, c):
            @pl.when(needs_zero(b))
            def _():
                block_copy(b).start()
            return c

        def zdrain(b, c):
            @pl.when(needs_zero(b))
            def _():
                block_copy(b).wait()
            return c

        lax.fori_loop(0, nb, zissue, 0)
        lax.fori_loop(0, nb, zdrain, 0)

    def issue(j, c):
        for k in range(TOP_K):
            _row_copy(u_ref, j, xb_out, pos_ref[k, j], sems.at[0]).start()
        return c

    lax.fori_loop(0, td, issue, 0, unroll=DMA_UNROLL)

    def drain(j, c):
        for k in range(TOP_K):
            _row_copy(u_ref, j, xb_out, pos_ref[k, j], sems.at[0]).wait()
        return c

    lax.fori_loop(0, td, drain, 0, unroll=DMA_UNROLL)


def _dispatch(blk_e, n_used, pos, u2, n_rows, td):
    t_rows = u2.shape[0] // RT
    nstep = t_rows // td
    return pl.pallas_call(
        _dispatch_body,
        out_shape=jax.ShapeDtypeStruct((n_rows * RT, LANES), F32),
        grid_spec=pltpu.PrefetchScalarGridSpec(
            num_scalar_prefetch=2,
            grid=(nstep,),
            in_specs=[
                pl.BlockSpec((SUBLANES, td), lambda i, be, nu: (0, i), memory_space=pltpu.SMEM),
                pl.BlockSpec((td * RT, LANES), lambda i, be, nu: (i, 0)),
            ],
            out_specs=pl.BlockSpec(memory_space=pl.ANY),
            scratch_shapes=[pltpu.VMEM((BM_MOE * RT, LANES), F32), pltpu.SemaphoreType.DMA((2,))],
        ),
        compiler_params=_cparams(("arbitrary",)),
        name="moe_dispatch",
    )(blk_e, n_used, pos, u2)


def _ffn_body(be_ref, nu_ref, x_ref, w1_ref, w3_ref, w2_ref, o_ref, w13_s, w2_s):
    i = pl.program_id(0)
    prev = be_ref[jnp.maximum(i - 1, 0)]
    fresh = (i == 0) | (be_ref[i] != prev)

    @pl.when(fresh & (i < nu_ref[0]))
    def _():
        w13_s[:, :D_EXP] = w1_ref[...].astype(BF16)
        w13_s[:, D_EXP:] = w3_ref[...].astype(BF16)
        w2_s[...] = w2_ref[...].astype(BF16)

    @pl.when(i < nu_ref[0])
    def _():
        x = _from_row_tiles(x_ref, BM_MOE).astype(BF16)
        a = jnp.dot(x, w13_s[...], preferred_element_type=F32)
        hid = _silu(a[:, :D_EXP]) * a[:, D_EXP:]
        _to_row_tiles(o_ref, jnp.dot(hid.astype(BF16), w2_s[...], preferred_element_type=F32))

    @pl.when(i >= nu_ref[0])
    def _():
        o_ref[...] = jnp.zeros_like(o_ref)


def _ffn(blk_e, n_used, xb, w1, w3, w2, layer):
    n_rows = xb.shape[0] // RT
    nb = n_rows // BM_MOE

    def last_used(i, nu):
        return jnp.maximum(jnp.minimum(i, nu[0] - 1), 0)

    def xmap(i, be, nu):
        return (last_used(i, nu), 0)

    def wmap(i, be, nu):
        return (layer, be[last_used(i, nu)], 0, 0)

    return pl.pallas_call(
        _ffn_body,
        out_shape=jax.ShapeDtypeStruct((n_rows * RT, LANES), F32),
        grid_spec=pltpu.PrefetchScalarGridSpec(
            num_scalar_prefetch=2,
            grid=(nb,),
            in_specs=[
                pl.BlockSpec((BM_MOE * RT, LANES), xmap),
                pl.BlockSpec((None, None, D, D_EXP), wmap),
                pl.BlockSpec((None, None, D, D_EXP), wmap),
                pl.BlockSpec((None, None, D_EXP, D), wmap),
            ],
            out_specs=pl.BlockSpec((BM_MOE * RT, LANES), lambda i, be, nu: (i, 0)),
            scratch_shapes=[pltpu.VMEM((D, 2 * D_EXP), BF16), pltpu.VMEM((D_EXP, D), BF16)],
        ),
        compiler_params=_cparams(("arbitrary",)),
        name="moe_ffn",
    )(blk_e, n_used, xb, w1, w3, w2)


def _combine_body(final, pos_ref, nxt_ref, yb_hbm, h_ref, info_ref, mod_ref, fn_ref, o_ref, ybuf, sems):
    i = pl.program_id(0)
    n = pl.num_programs(0)
    td = h_ref.shape[0]
    slot = i % 2

    def gather(p_ref, s, wait):
        def body(j, c):
            for k in range(TOP_K):
                cp = _row_copy(yb_hbm, p_ref[k, j], ybuf.at[s, k], j, sems.at[s])
                cp.wait() if wait else cp.start()
            return c

        lax.fori_loop(0, td, body, 0, unroll=DMA_UNROLL)

    @pl.when(i == 0)
    def _():
        gather(pos_ref, slot, False)

    @pl.when(i + 1 < n)
    def _():
        gather(nxt_ref, 1 - slot, False)

    gather(pos_ref, slot, True)
    info = info_ref[...]
    y = (info[:, 4:5] * _from_row_tiles(ybuf.at[slot, 0], td)
         + info[:, 5:6] * _from_row_tiles(ybuf.at[slot, 1], td))
    h2 = h_ref[...] + mod_ref[5:6, :] * y
    if final:
        h2 = h2 * lax.rsqrt(jnp.mean(h2 * h2, axis=-1, keepdims=True) + EPS) * fn_ref[...]
    o_ref[...] = h2


def _combine(pos, yb, h1, info, mod, fnorm, seq, td, final):
    t_rows = h1.shape[0]
    per_b = seq // td
    nstep = t_rows // td
    row = lambda i: (i, 0)
    return pl.pallas_call(
        functools.partial(_combine_body, final),
        out_shape=jax.ShapeDtypeStruct((t_rows, D), F32),
        grid=(nstep,),
        in_specs=[
            pl.BlockSpec((SUBLANES, td), lambda i: (0, i), memory_space=pltpu.SMEM),
            pl.BlockSpec((SUBLANES, td), lambda i: (0, jnp.minimum(i + 1, nstep - 1)),
                         memory_space=pltpu.SMEM),
            pl.BlockSpec(memory_space=pl.ANY),
            pl.BlockSpec((td, D), row),
            pl.BlockSpec((td, LANES), row),
            pl.BlockSpec((None, 6, D), lambda i: (i // per_b, 0, 0)),
            pl.BlockSpec((1, D), lambda i: (0, 0)),
        ],
        out_specs=pl.BlockSpec((td, D), row),
        scratch_shapes=[pltpu.VMEM((2, TOP_K, td * RT, LANES), F32), pltpu.SemaphoreType.DMA((2,))],
        compiler_params=_cparams(("arbitrary",)),
        name="moe_combine",
    )(pos, pos, yb, h1, info, mod, fnorm.reshape(1, D))


def _moe(u2, info, counts, h1, mod, w1, w3, w2, layer, fnorm, seq, final):
    t_rows = h1.shape[0]
    td = min(TD_MOE, seq)
    nb = (t_rows * TOP_K) // BM_MOE + N_EXP
    n_rows = nb * BM_MOE
    cnt = counts[0, :N_EXP].astype(I32)
    padded = (cnt + BM_MOE - 1) // BM_MOE * BM_MOE
    end_p = jnp.cumsum(padded)
    start_p = end_p - padded
    sp_row = jnp.zeros((1, LANES), F32).at[0, :N_EXP].set(start_p.astype(F32))
    blk_e = jnp.minimum(jnp.sum(jnp.arange(nb, dtype=I32)[:, None] * BM_MOE >= end_p[None, :], axis=1),
                        N_EXP - 1).astype(I32)
    n_used = (end_p[-1:] // BM_MOE).astype(I32)
    pos = _moe_pos(info, sp_row)
    xb = _dispatch(blk_e, n_used, pos, u2, n_rows, td)
    yb = _ffn(blk_e, n_used, xb, w1, w3, w2, layer)
    return _combine(pos, yb, h1, info, mod, fnorm, seq, td, final)


def _pad_cols(w, n):
    return jnp.pad(w, ((0, 0), (0, n - w.shape[1])))


def _pad_heads(w):
    w = w.reshape(w.shape[:-1] + (NH_B, DK_B))
    w = jnp.pad(w, [(0, 0)] * (w.ndim - 1) + [(0, DKP_B - DK_B)])
    return w.reshape(w.shape[:-2] + (KP_B,))


def _block_diag(w):
    nh, hw, _ = w.shape
    eye = jnp.eye(nh, dtype=w.dtype)
    return (eye[:, None, :, None] * w[:, :, None, :]).reshape(nh * hw, nh * hw)


def _router_weights(w_grp, b_grp, w_rt, b_rt):
    w = _pad_cols(jnp.concatenate([w_rt, w_grp], axis=1), LANES)
    b = _pad_cols(jnp.concatenate([b_rt, b_grp]).reshape(1, -1), LANES)
    w_hi = w.astype(BF16)
    w_lo = (w - w_hi.astype(F32)).astype(BF16)
    return w_hi, w_lo, b


def kernel(x, c, norm1, norm2, w_ada, b_ada, w_in_ab, conv_a_w, conv_a_b, rg_wa, rg_ba, rg_wx, rg_bx, rg_lam, gla_wg2, gla_bg2, gla_norm, w_out_ab, w_in_c, conv_c_w, dn_a_log, dn_dt_bias, dn_norm, w_out_c, moe_w_grp, moe_b_grp, moe_w_rt, moe_b_rt, moe_w1, moe_w3, moe_w2, final_norm):
    bsz, seq, _ = x.shape
    depth = w_ada.shape[0]
    t_rows = bsz * seq
    mod_all = _ada_mod(c, w_ada, b_ada)
    h = x.reshape(t_rows, D)
    for layer in range(depth):
        mod = mod_all[layer]
        j = layer // 2
        if layer % 2 == 0:
            wi = w_in_ab[j]
            o_q, o_k, o_v, o_gl = 2 * W_A, 2 * W_A + K_B, 2 * W_A + 2 * K_B, 2 * W_A + 2 * K_B + 2 * V_B
            w_in = jnp.concatenate([
                wi[:, :o_q], _pad_heads(wi[:, o_q:o_k]), _pad_heads(wi[:, o_k:o_v]), wi[:, o_v:o_gl],
                _pad_cols(wi[:, o_gl:], LANES)], axis=1).astype(BF16)
            xa, ga, q, k, v, og, gl = _inproj(h, mod, norm1[layer], w_in,
                                              (W_A, W_A, KP_B, KP_B, V_B, V_B, LANES), seq)
            wg_bd = jnp.concatenate([_block_diag(rg_wa[j]), _block_diag(rg_wx[j])], axis=1).astype(BF16)
            bg = jnp.concatenate([rg_ba[j], rg_bx[j]])
            ya = _rglru(xa, ga, conv_a_w[j], conv_a_b[j], wg_bd, bg, rg_lam[j], bsz, seq)
            wg2p = jnp.pad(_pad_heads(gla_wg2[j]), ((0, LANES - R_GATE), (0, 0)))
            bg2p = _pad_heads(gla_bg2[j].reshape(1, K_B))
            ob = _gla(q, k, v, og, gl, wg2p, bg2p, gla_norm[j], bsz, seq)
            ys = (ya, ob)
            ws = (w_out_ab[j][:W_A].astype(BF16), w_out_ab[j][W_A:].astype(BF16))
        else:
            p_c = w_in_c.shape[2]
            w_in = _pad_cols(w_in_c[j], p_c - 2 * NH_C + LANES).astype(BF16)
            q, k, v, z, ba = _inproj(h, mod, norm1[layer], w_in, (W_C, W_C, W_C, W_C, LANES), seq)
            cw = conv_c_w[j]
            lanes_c = jnp.zeros((1, LANES), F32)
            alog = lanes_c.at[0, NH_C:2 * NH_C].set(dn_a_log[j])
            dtb = lanes_c.at[0, NH_C:2 * NH_C].set(dn_dt_bias[j])
            o = _delta(q, k, v, z, ba, cw[:, :W_C], cw[:, W_C:2 * W_C], cw[:, 2 * W_C:], alog, dtb,
                       dn_norm[j].reshape(1, DH_C), bsz, seq)
            ys = (o,)
            ws = (w_out_c[j].astype(BF16),)
        rw = _router_weights(moe_w_grp[layer], moe_b_grp[layer], moe_w_rt[layer], moe_b_rt[layer])
        h1, u2, info, counts = _outproj_router(ys, ws, h, mod, norm2[layer], *rw, seq)
        h = _moe(u2, info, counts, h1, mod, moe_w1, moe_w3, moe_w2, layer, final_norm, seq,
                 final=layer == depth - 1)
    return h.reshape(bsz, seq, D)
```

```python
import functools

import jax
import jax.numpy as jnp
from jax import lax
from jax.experimental import pallas as pl
from jax.experimental.pallas import tpu as pltpu

F32 = jnp.float32
BF16 = jnp.bfloat16
I32 = jnp.int32
HI = lax.Precision.HIGHEST

D = 1024
CHUNK = 64
SUB = 16
CONV_W = 4
EPS = 1e-6
LANES = 128
SUBLANES = 8

W_A = D // 2
NH_A = 8
HW_A = W_A // NH_A
RG_C = 8.0
NH_B = 4
DK_B = 64
DV_B = 128
K_B = NH_B * DK_B
V_B = NH_B * DV_B
DKP_B = LANES
KP_B = NH_B * DKP_B
R_GATE = 16
GATE_NORM = 16.0

NH_C = 8
DH_C = 128
W_C = NH_C * DH_C

N_GROUPS = 4
EPG = 8
N_EXP = N_GROUPS * EPG
TOP_K = 2
D_EXP = 512

VMEM_LIMIT = 56 * 1024 * 1024

TM_PROJ = 512
TS_ROUTER = 512
TT_SEQ = 256
BM_MOE = 512
TD_MOE = 256
TP_MOE = 2048
DMA_UNROLL = 8


def _cparams(sem):
    return pltpu.CompilerParams(dimension_semantics=sem, vmem_limit_bytes=VMEM_LIMIT)


def _softplus(x):
    return jnp.maximum(x, 0.0) + jnp.log1p(jnp.exp(-jnp.abs(x)))


def _sigmoid(x):
    return jax.nn.sigmoid(x)


def _silu(x):
    return x * jax.nn.sigmoid(x)


def _rms_mod(h, g, sc, sh):
    y = h * lax.rsqrt(jnp.mean(h * h, axis=-1, keepdims=True) + EPS)
    return y * g * (1.0 + sc) + sh


def _dot(a, b):
    return jnp.dot(a.astype(BF16), b.astype(BF16), preferred_element_type=F32)


def _dot_nt(a, b):
    return lax.dot_general(a.astype(BF16), b.astype(BF16), (((1,), (1,)), ((), ())),
                           preferred_element_type=F32)


def _dot_tn(a, b):
    return lax.dot_general(a.astype(BF16), b.astype(BF16), (((0,), (0,)), ((), ())),
                           preferred_element_type=F32)


def _dot_hi(a, b):
    return jnp.dot(a, b, precision=HI, preferred_element_type=F32)


def _ada_body(c_ref, w_ref, b_ref, o_ref):
    o_ref[...] = _dot_hi(_silu(c_ref[...]), w_ref[...]) + b_ref[...]


def _ada_mod(c, w_ada, b_ada):
    depth, _, n6 = w_ada.shape
    bsz = c.shape[0]
    nt = 1536
    c8 = jnp.zeros((SUBLANES, D), F32).at[:bsz].set(c)
    out = pl.pallas_call(
        _ada_body,
        out_shape=jax.ShapeDtypeStruct((depth, SUBLANES, n6), F32),
        grid=(depth, n6 // nt),
        in_specs=[
            pl.BlockSpec((SUBLANES, D), lambda l, j: (0, 0)),
            pl.BlockSpec((None, D, nt), lambda l, j: (l, 0, j)),
            pl.BlockSpec((None, 1, nt), lambda l, j: (l, 0, j)),
        ],
        out_specs=pl.BlockSpec((None, SUBLANES, nt), lambda l, j: (l, 0, j)),
        compiler_params=_cparams(("arbitrary", "arbitrary")),
        name="ada_mod",
    )(c8, w_ada, b_ada.reshape(depth, 1, n6))
    return out[:, :bsz].reshape(depth, bsz, 6, D)


def _inproj_body(widths, h_ref, mod_ref, g_ref, w_ref, *out_refs):
    u = _rms_mod(h_ref[...], g_ref[...], mod_ref[1:2, :], mod_ref[0:1, :]).astype(BF16)
    off = 0
    for o_ref, wd in zip(out_refs, widths):
        o_ref[...] = jnp.dot(u, w_ref[:, off:off + wd], preferred_element_type=F32)
        off += wd


def _inproj(h, mod, g, w_bf16, widths, seq):
    t_rows = h.shape[0]
    tm = min(TM_PROJ, seq)
    per_b = seq // tm
    p = w_bf16.shape[1]
    assert sum(widths) == p
    return pl.pallas_call(
        functools.partial(_inproj_body, widths),
        out_shape=[jax.ShapeDtypeStruct((t_rows, wd), F32) for wd in widths],
        grid=(t_rows // tm,),
        in_specs=[
            pl.BlockSpec((tm, D), lambda i: (i, 0)),
            pl.BlockSpec((None, 6, D), lambda i: (i // per_b, 0, 0)),
            pl.BlockSpec((1, D), lambda i: (0, 0)),
            pl.BlockSpec((D, p), lambda i: (0, 0)),
        ],
        out_specs=[pl.BlockSpec((tm, wd), lambda i: (i, 0)) for wd in widths],
        compiler_params=_cparams(("arbitrary",)),
        name="inproj",
    )(h, mod, g.reshape(1, D), w_bf16)


def _causal_conv(x_ref, cw_ref, xbuf, first):
    tt = x_ref.shape[0]

    @pl.when(first)
    def _():
        xbuf[0:SUBLANES, :] = jnp.zeros((SUBLANES, xbuf.shape[1]), F32)

    xbuf[SUBLANES:SUBLANES + tt, :] = x_ref[...]
    acc = None
    for w in range(CONV_W):
        lo = SUBLANES - (CONV_W - 1) + w
        term = cw_ref[w:w + 1, :] * xbuf[lo:lo + tt, :]
        acc = term if acc is None else acc + term
    xbuf[0:SUBLANES, :] = xbuf[tt:tt + SUBLANES, :]
    return acc


def _rglru_body(xa_ref, ga_ref, cw_ref, cb_ref, wg_ref, bg_ref, lam_ref, o_ref, xbuf, hc):
    t = pl.program_id(1)
    tt = xa_ref.shape[0]

    @pl.when(t == 0)
    def _():
        hc[...] = jnp.zeros_like(hc)

    xc = _causal_conv(xa_ref, cw_ref, xbuf, t == 0) + cb_ref[...]
    gates = _dot(xc, wg_ref[...]) + bg_ref[...]
    r = _sigmoid(gates[:, :W_A])
    i = _sigmoid(gates[:, W_A:])
    log_a = -RG_C * r * _softplus(-lam_ref[...])
    a = jnp.exp(log_a)
    b = jnp.sqrt(1.0 - a * a) * (i * xc)
    sub = lax.broadcasted_iota(I32, (tt, W_A), 0) % SUBLANES
    s = 1
    while s < SUBLANES:
        a_sh = pltpu.roll(a, s, 0)
        b_sh = pltpu.roll(b, s, 0)
        valid = sub >= s
        b = jnp.where(valid, a * b_sh + b, b)
        a = jnp.where(valid, a * a_sh, a)
        s *= 2
    carry = hc[0:1, :]
    groups = []
    for grp in range(tt // SUBLANES):
        rows = slice(grp * SUBLANES, (grp + 1) * SUBLANES)
        hg = b[rows] + a[rows] * carry
        carry = hg[SUBLANES - 1:SUBLANES, :]
        groups.append(hg)
    h = jnp.concatenate(groups, axis=0)
    hc[...] = jnp.broadcast_to(carry, hc.shape)
    o_ref[...] = jax.nn.gelu(ga_ref[...], approximate=True) * h


def _rglru(xa, ga, conv_w, conv_b, wg_bd, bg, lam, bsz, seq):
    tt = min(TT_SEQ, seq)
    per_b = seq // tt
    row = lambda b, t: (b * per_b + t, 0)
    const = lambda b, t: (0, 0)
    return pl.pallas_call(
        _rglru_body,
        out_shape=jax.ShapeDtypeStruct(xa.shape, F32),
        grid=(bsz, per_b),
        in_specs=[
            pl.BlockSpec((tt, W_A), row),
            pl.BlockSpec((tt, W_A), row),
            pl.BlockSpec((CONV_W, W_A), const),
            pl.BlockSpec((1, W_A), const),
            pl.BlockSpec((W_A, 2 * W_A), const),
            pl.BlockSpec((1, 2 * W_A), const),
            pl.BlockSpec((1, W_A), const),
        ],
        out_specs=pl.BlockSpec((tt, W_A), row),
        scratch_shapes=[pltpu.VMEM((tt + SUBLANES, W_A), F32), pltpu.VMEM((SUBLANES, W_A), F32)],
        compiler_params=_cparams(("arbitrary", "arbitrary")),
        name="rglru",
    )(xa, ga, conv_w, conv_b.reshape(1, W_A), wg_bd, bg.reshape(1, 2 * W_A), lam.reshape(1, W_A))


def _gla_body(q_ref, k_ref, v_ref, og_ref, gl_ref, wg_ref, bg_ref, gn_ref, o_ref, st_ref):
    t = pl.program_id(0)
    nbat, tt = q_ref.shape[0], q_ref.shape[1]

    @pl.when(t == 0)
    def _():
        st_ref[...] = jnp.zeros_like(st_ref)

    ri = lax.broadcasted_iota(I32, (CHUNK, CHUNK), 0)
    ci = lax.broadcasted_iota(I32, (CHUNK, CHUNK), 1)
    tri_incl = (ri >= ci).astype(F32)

    def chunk(c, carry):
        r0 = pl.multiple_of(c * CHUNK, CHUNK)
        rows = pl.ds(r0, CHUNK)
        heads = range(nbat * NH_B)
        blocks = [(b * SUB, (b + 1) * SUB) for b in range(CHUNK // SUB)]
        qh, kh, gh, egh, vh, ogh = [], [], [], [], [], []
        for b in range(nbat):
            x = _dot_hi(gl_ref[b, rows, :], wg_ref[...]) + bg_ref[...]
            lg = -_softplus(-x) * (1.0 / GATE_NORM)
            g = _dot_hi(tri_incl, lg)
            eg = jnp.exp(g)
            q_all = q_ref[b, rows, :] * (DK_B ** -0.5)
            k_all = k_ref[b, rows, :]
            v_all = v_ref[b, rows, :]
            og_all = og_ref[b, rows, :]
            for hh in range(NH_B):
                ks_ = slice(hh * DKP_B, (hh + 1) * DKP_B)
                vs_ = slice(hh * DV_B, (hh + 1) * DV_B)
                qh.append(q_all[:, ks_])
                kh.append(k_all[:, ks_])
                gh.append(g[:, ks_])
                egh.append(eg[:, ks_])
                vh.append(v_all[:, vs_])
                ogh.append(og_all[:, vs_])
        st = [st_ref[h] for h in heads]
        g_last = [gh[h][CHUNK - 1:CHUNK, :] for h in heads]
        att = [[None] * len(blocks) for _ in heads]
        for h in heads:
            for b, (lo, hi) in enumerate(blocks):
                ref_g = gh[h][lo:lo + 1, :]
                qb = qh[h][lo:hi] * jnp.exp(gh[h][lo:hi] - ref_g)
                kb = kh[h][:hi] * jnp.exp(ref_g - gh[h][:hi])
                causal = (lax.broadcasted_iota(I32, (SUB, hi), 1)
                          <= lax.broadcasted_iota(I32, (SUB, hi), 0) + lo)
                att[h][b] = jnp.where(causal, _dot_nt(qb, kb), 0.0)
        o_inter = [_dot_nt(qh[h] * egh[h], st[h]) for h in heads]
        upd = [_dot_tn(vh[h], kh[h] * jnp.exp(g_last[h] - gh[h])) for h in heads]
        o_intra = [[_dot(att[h][b], vh[h][:hi]) for b, (lo, hi) in enumerate(blocks)] for h in heads]
        outs = []
        for h in heads:
            st_ref[h] = st[h] * jnp.exp(g_last[h]) + upd[h]
            o = o_inter[h] + jnp.concatenate(o_intra[h], axis=0)
            o = o * lax.rsqrt(jnp.mean(o * o, axis=-1, keepdims=True) + EPS) * gn_ref[...]
            outs.append(o * _silu(ogh[h]))
        for b in range(nbat):
            o_ref[b, rows, :] = jnp.concatenate(outs[b * NH_B:(b + 1) * NH_B], axis=1)
        return carry

    lax.fori_loop(0, tt // CHUNK, chunk, 0)


def _gla(q, k, v, og, gl, wg2p, bg2p, gnorm, bsz, seq):
    tt = min(TT_SEQ, seq)
    row = lambda t: (0, t, 0)
    const = lambda t: (0, 0)
    r3 = lambda a: a.reshape(bsz, seq, a.shape[-1])
    out = pl.pallas_call(
        _gla_body,
        out_shape=jax.ShapeDtypeStruct((bsz, seq, V_B), F32),
        grid=(seq // tt,),
        in_specs=[
            pl.BlockSpec((bsz, tt, KP_B), row),
            pl.BlockSpec((bsz, tt, KP_B), row),
            pl.BlockSpec((bsz, tt, V_B), row),
            pl.BlockSpec((bsz, tt, V_B), row),
            pl.BlockSpec((bsz, tt, LANES), row),
            pl.BlockSpec((LANES, KP_B), const),
            pl.BlockSpec((1, KP_B), const),
            pl.BlockSpec((1, DV_B), const),
        ],
        out_specs=pl.BlockSpec((bsz, tt, V_B), row),
        scratch_shapes=[pltpu.VMEM((bsz * NH_B, DV_B, DKP_B), F32)],
        compiler_params=_cparams(("arbitrary",)),
        name="gla",
    )(r3(q), r3(k), r3(v), r3(og), r3(gl), wg2p, bg2p, gnorm.reshape(1, DV_B))
    return out.reshape(bsz * seq, V_B)


def _unit_lower_solve(lows, rhss, ri, ci):
    n = range(len(lows))
    same = (ri // SUB) == (ci // SUB)
    eye = (ri == ci).astype(F32)
    dg = [jnp.where(same, lows[h], 0.0) for h in n]
    off = [lows[h] - dg[h] for h in n]
    tinv = [eye - dg[h] for h in n]
    pw = dg
    e = 2
    while e < SUB:
        pw = [_dot(pw[h], pw[h]) for h in n]
        tinv = [tinv[h] + _dot(tinv[h], pw[h]) for h in n]
        e *= 2
    nmat = [_dot(tinv[h], off[h]) for h in n]
    x = [_dot(tinv[h], rhss[h]) for h in n]
    sol = [[x[h][0:SUB]] for h in n]
    for blk in range(1, CHUNK // SUB):
        lo, hi = blk * SUB, (blk + 1) * SUB
        for h in n:
            prev = jnp.concatenate(sol[h], axis=0)
            sol[h].append(x[h][lo:hi] - _dot(nmat[h][lo:hi, :lo], prev))
    return [jnp.concatenate(sol[h], axis=0) for h in n]


def _delta_body(q_ref, k_ref, v_ref, z_ref, ba_ref, cq_ref, ck_ref, cv_ref, alog_ref, dtb_ref, dn_ref,
                o_ref, qbuf, kbuf, vbuf, qs, ks, vs, st_ref):
    t = pl.program_id(0)
    nbat, tt = q_ref.shape[0], q_ref.shape[1]

    @pl.when(t == 0)
    def _():
        st_ref[...] = jnp.zeros_like(st_ref)

    for b in range(nbat):
        qs[b] = _silu(_causal_conv(q_ref.at[b], cq_ref, qbuf.at[b], t == 0))
        ks[b] = _silu(_causal_conv(k_ref.at[b], ck_ref, kbuf.at[b], t == 0))
        vs[b] = _silu(_causal_conv(v_ref.at[b], cv_ref, vbuf.at[b], t == 0))

    ri = lax.broadcasted_iota(I32, (CHUNK, CHUNK), 0)
    ci = lax.broadcasted_iota(I32, (CHUNK, CHUNK), 1)
    tri_incl = (ri >= ci).astype(F32)
    incl = ri >= ci
    strict = ri > ci

    def chunk(c, carry):
        r0 = pl.multiple_of(c * CHUNK, CHUNK)
        rows = pl.ds(r0, CHUNK)
        heads = range(nbat * NH_C)
        qh, kh, vh, zh, beta, gcol, egcol, g_last, decay = [], [], [], [], [], [], [], [], []
        for b in range(nbat):
            ba = ba_ref[b, rows, :]
            beta_all = _sigmoid(ba)
            g_all = -jnp.exp(alog_ref[...]) * _softplus(ba + dtb_ref[...])
            gcum = _dot_hi(tri_incl, g_all)
            gcum_t = gcum.T
            egc = jnp.exp(gcum)
            q_all, k_all, v_all, z_all = qs[b, rows, :], ks[b, rows, :], vs[b, rows, :], z_ref[b, rows, :]
            for hh in range(NH_C):
                hs = slice(hh * DH_C, (hh + 1) * DH_C)
                q_, k_ = q_all[:, hs], k_all[:, hs]
                qh.append(q_ * lax.rsqrt(jnp.sum(q_ * q_, axis=-1, keepdims=True) + EPS) * (DH_C ** -0.5))
                kh.append(k_ * lax.rsqrt(jnp.sum(k_ * k_, axis=-1, keepdims=True) + EPS))
                vh.append(v_all[:, hs])
                zh.append(z_all[:, hs])
                beta.append(beta_all[:, hh:hh + 1])
                gcol.append(gcum[:, NH_C + hh:NH_C + hh + 1])
                egcol.append(egc[:, NH_C + hh:NH_C + hh + 1])
                g_last.append(gcum[CHUNK - 1:CHUNK, NH_C + hh:NH_C + hh + 1])
                grow = gcum_t[NH_C + hh:NH_C + hh + 1, :]
                decay.append(jnp.exp(jnp.where(incl, gcol[-1] - grow, -jnp.inf)))
        kbeta = [kh[h] * beta[h] for h in heads]
        kq = [_dot_nt(jnp.concatenate([kbeta[h], qh[h]], axis=0), kh[h]) for h in heads]
        low = [jnp.where(strict, kq[h][:CHUNK] * decay[h], 0.0) for h in heads]
        rhs = [jnp.concatenate([vh[h] * beta[h], kbeta[h] * egcol[h]], axis=1) for h in heads]
        sol = _unit_lower_solve(low, rhs, ri, ci)
        st = [st_ref[h] for h in heads]
        ws = [_dot(jnp.concatenate([sol[h][:, DH_C:], qh[h] * egcol[h]], axis=0), st[h]) for h in heads]
        o_inter = [ws[h][CHUNK:] for h in heads]
        v_new = [sol[h][:, :DH_C] - ws[h][:CHUNK] for h in heads]
        o_intra = [_dot(kq[h][CHUNK:] * decay[h], v_new[h]) for h in heads]
        upd = [_dot_tn(kh[h] * jnp.exp(g_last[h] - gcol[h]), v_new[h]) for h in heads]
        outs = []
        for h in heads:
            st_ref[h] = st[h] * jnp.exp(g_last[h]) + upd[h]
            o = o_inter[h] + o_intra[h]
            o = o * lax.rsqrt(jnp.mean(o * o, axis=-1, keepdims=True) + EPS) * dn_ref[...]
            outs.append(o * _silu(zh[h]))
        for b in range(nbat):
            o_ref[b, rows, :] = jnp.concatenate(outs[b * NH_C:(b + 1) * NH_C], axis=1)
        return carry

    lax.fori_loop(0, tt // CHUNK, chunk, 0)


def _delta(q, k, v, z, ba, cq, ck, cv, alog, dtb, dnorm, bsz, seq):
    tt = min(TT_SEQ, seq)
    row = lambda t: (0, t, 0)
    const = lambda t: (0, 0)
    big = pl.BlockSpec((bsz, tt, W_C), row)
    cw = pl.BlockSpec((CONV_W, W_C), const)
    vec = pl.BlockSpec((1, LANES), const)
    r3 = lambda a: a.reshape(bsz, seq, a.shape[-1])
    out = pl.pallas_call(
        _delta_body,
        out_shape=jax.ShapeDtypeStruct((bsz, seq, W_C), F32),
        grid=(seq // tt,),
        in_specs=[big, big, big, big, pl.BlockSpec((bsz, tt, LANES), row), cw, cw, cw, vec, vec, vec],
        out_specs=big,
        scratch_shapes=[pltpu.VMEM((bsz, tt + SUBLANES, W_C), F32)] * 3 + [pltpu.VMEM((bsz, tt, W_C), F32)] * 3
        + [pltpu.VMEM((bsz * NH_C, DH_C, DH_C), F32)],
        compiler_params=_cparams(("arbitrary",)),
        name="delta",
    )(r3(q), r3(k), r3(v), r3(z), r3(ba), cq, ck, cv, alog, dtb, dnorm)
    return out.reshape(bsz * seq, W_C)


def _outproj_router_body(n_parts, *refs):
    y_refs = refs[:n_parts]
    w_refs = refs[n_parts:2 * n_parts]
    (h_ref, mod_ref, g2_ref, wr_hi_ref, wr_lo_ref, br_ref,
     h1_ref, u2_ref, info_ref, cnt_ref, carry) = refs[2 * n_parts:]
    i = pl.program_id(0)
    tm = h_ref.shape[0]

    @pl.when(i == 0)
    def _():
        carry[...] = jnp.zeros_like(carry)

    ts = min(TS_ROUTER, tm)
    count = carry[0:1, :]
    for r0 in range(0, tm, ts):
        count = _route_rows(slice(r0, r0 + ts), count, y_refs, w_refs, h_ref, mod_ref, g2_ref, wr_hi_ref,
                            wr_lo_ref, br_ref, h1_ref, u2_ref, info_ref)
    carry[...] = jnp.broadcast_to(count, carry.shape)
    cnt_ref[...] = carry[...]


def _route_rows(rows, count, y_refs, w_refs, h_ref, mod_ref, g2_ref, wr_hi_ref, wr_lo_ref, br_ref,
                h1_ref, u2_ref, info_ref):
    tm = rows.stop - rows.start
    m = None
    for y_ref, w_ref in zip(y_refs, w_refs):
        part = _dot(y_ref[rows, :], w_ref[...])
        m = part if m is None else m + part
    h1 = h_ref[rows, :] + mod_ref[2:3, :] * m
    h1_ref[rows, :] = h1
    u2 = _rms_mod(h1, g2_ref[...], mod_ref[4:5, :], mod_ref[3:4, :])
    _to_row_tiles(u2_ref, u2, rows.start)

    lane = lax.broadcasted_iota(I32, (tm, LANES), 1)
    neg = -jnp.inf
    u_hi = u2.astype(BF16)
    u_lo = (u2 - u_hi.astype(F32)).astype(BF16)
    w_hi = wr_hi_ref[...]
    logits = (jnp.dot(u_hi, w_hi, preferred_element_type=F32)
              + jnp.dot(u_hi, wr_lo_ref[...], preferred_element_type=F32)
              + jnp.dot(u_lo, w_hi, preferred_element_type=F32)) + br_ref[...]
    lgp = jnp.where((lane >= N_EXP) & (lane < N_EXP + N_GROUPS), logits, neg)
    gmax = jnp.max(lgp, axis=-1, keepdims=True)
    pg_top = 1.0 / jnp.sum(jnp.exp(lgp - gmax), axis=-1, keepdims=True)
    g_idx = jnp.min(jnp.where(lgp == gmax, lane, LANES), axis=-1, keepdims=True) - N_EXP
    in_group = (lane // EPG == g_idx) & (lane < N_EXP)
    le = jnp.where(in_group, logits, neg)
    emax = jnp.max(le, axis=-1, keepdims=True)
    pe = jnp.exp(le - emax)
    pe = jnp.where(in_group, pe / jnp.sum(pe, axis=-1, keepdims=True), -1.0)
    p0 = jnp.max(pe, axis=-1, keepdims=True)
    e0 = jnp.min(jnp.where(pe == p0, lane, LANES), axis=-1, keepdims=True)
    pe1 = jnp.where(lane == e0, -1.0, pe)
    p1 = jnp.max(pe1, axis=-1, keepdims=True)
    e1 = jnp.min(jnp.where(pe1 == p1, lane, LANES), axis=-1, keepdims=True)
    den = p0 + p1
    w0 = pg_top * p0 / den
    w1 = pg_top * p1 / den
    hot = ((lane == e0) | (lane == e1)).astype(F32)
    ri = lax.broadcasted_iota(I32, (tm, tm), 0)
    ci = lax.broadcasted_iota(I32, (tm, tm), 1)
    before = _dot((ri > ci).astype(F32), hot) + count
    r0 = jnp.sum(jnp.where(lane == e0, before, 0.0), axis=-1, keepdims=True)
    r1 = jnp.sum(jnp.where(lane == e1, before, 0.0), axis=-1, keepdims=True)
    info = jnp.where(lane == 0, e0.astype(F32), 0.0)
    info = jnp.where(lane == 1, e1.astype(F32), info)
    info = jnp.where(lane == 2, r0, info)
    info = jnp.where(lane == 3, r1, info)
    info = jnp.where(lane == 4, w0, info)
    info = jnp.where(lane == 5, w1, info)
    info_ref[rows, :] = info
    return count + jnp.sum(hot, axis=0, keepdims=True)


def _outproj_router(ys, ws, h, mod, g2, wr_hi, wr_lo, br, seq):
    t_rows = h.shape[0]
    tm = min(TM_PROJ, seq)
    per_b = seq // tm
    n = len(ys)
    row = lambda i: (i, 0)
    const = lambda i: (0, 0)
    in_specs = [pl.BlockSpec((tm, y.shape[1]), row) for y in ys]
    in_specs += [pl.BlockSpec(w.shape, const) for w in ws]
    in_specs += [
        pl.BlockSpec((tm, D), row),
        pl.BlockSpec((None, 6, D), lambda i: (i // per_b, 0, 0)),
        pl.BlockSpec((1, D), const),
        pl.BlockSpec((D, LANES), const),
        pl.BlockSpec((D, LANES), const),
        pl.BlockSpec((1, LANES), const),
    ]
    return pl.pallas_call(
        functools.partial(_outproj_router_body, n),
        out_shape=[
            jax.ShapeDtypeStruct((t_rows, D), F32),
            jax.ShapeDtypeStruct((t_rows * RT, LANES), F32),
            jax.ShapeDtypeStruct((t_rows, LANES), F32),
            jax.ShapeDtypeStruct((SUBLANES, LANES), F32),
        ],
        grid=(t_rows // tm,),
        in_specs=in_specs,
        out_specs=[
            pl.BlockSpec((tm, D), row),
            pl.BlockSpec((tm * RT, LANES), row),
            pl.BlockSpec((tm, LANES), row),
            pl.BlockSpec((SUBLANES, LANES), const),
        ],
        scratch_shapes=[pltpu.VMEM((SUBLANES, LANES), F32)],
        compiler_params=_cparams(("arbitrary",)),
        name="outproj_router",
    )(*ys, *ws, h, mod, g2.reshape(1, D), wr_hi, wr_lo, br)


RT = D // LANES
assert RT == SUBLANES


def _to_row_tiles(ref, x, row0=0):
    n = x.shape[0]
    for c in range(RT):
        ref[pl.ds(row0 * RT + c, n, stride=RT), :] = x[:, c * LANES:(c + 1) * LANES]


def _from_row_tiles(ref, n):
    return jnp.concatenate([ref[pl.ds(c, n, stride=RT), :] for c in range(RT)], axis=1)


def _row_copy(src, src_row, dst, dst_row, sem):
    return pltpu.make_async_copy(src.at[pl.ds(pl.multiple_of(src_row * RT, RT), RT)],
                                 dst.at[pl.ds(pl.multiple_of(dst_row * RT, RT), RT)], sem)


def _pos_body(info_ref, sp_ref, o_ref):
    info = info_ref[...]
    td = info.shape[0]
    lane = lax.broadcasted_iota(I32, (td, LANES), 1)
    sp = sp_ref[...]
    packed = jnp.zeros((td, LANES), F32)
    for k in range(TOP_K):
        e = info[:, k:k + 1].astype(I32)
        start = jnp.sum(jnp.where(lane == e, sp, 0.0), axis=-1, keepdims=True)
        packed = jnp.where(lane == k, start + info[:, TOP_K + k:TOP_K + k + 1], packed)
    pick = (lax.broadcasted_iota(I32, (SUBLANES, LANES), 0)
            == lax.broadcasted_iota(I32, (SUBLANES, LANES), 1)).astype(F32)
    o_ref[...] = lax.dot_general(pick, packed, (((1,), (1,)), ((), ())), precision=HI,
                                 preferred_element_type=F32).astype(I32)


def _moe_pos(info, sp_row):
    t_rows = info.shape[0]
    tp = min(TP_MOE, t_rows)
    return pl.pallas_call(
        _pos_body,
        out_shape=jax.ShapeDtypeStruct((SUBLANES, t_rows), I32),
        grid=(t_rows // tp,),
        in_specs=[pl.BlockSpec((tp, LANES), lambda i: (i, 0)), pl.BlockSpec((1, LANES), lambda i: (0, 0))],
        out_specs=pl.BlockSpec((SUBLANES, tp), lambda i: (0, i)),
        compiler_params=_cparams(("arbitrary",)),
        name="moe_pos",
    )(info, sp_row)


def _dispatch_body(be_ref, nu_ref, pos_ref, u_ref, xb_out, zero_s, sems):
    i = pl.program_id(0)
    td = u_ref.shape[0] // RT
    nb = be_ref.shape[0]

    @pl.when(i == 0)
    def _():
        zero_s[...] = jnp.zeros_like(zero_s)

        def needs_zero(b):
            return (b >= nu_ref[0] - 1) | (be_ref[jnp.minimum(b + 1, nb - 1)] != be_ref[b])

        def block_copy(b):
            return pltpu.make_async_copy(zero_s, xb_out.at[pl.ds(b * (BM_MOE * RT), BM_MOE * RT)], sems.at[1])

        def zissue(b, c):
            @pl.when(needs_zero(b))
            def _():
                block_copy(b).start()
            return c

        def zdrain(b, c):
            @pl.when(needs_zero(b))
            def _():
                block_copy(b).wait()
            return c

        lax.fori_loop(0, nb, zissue, 0)
        lax.fori_loop(0, nb, zdrain, 0)

    def issue(j, c):
        for k in range(TOP_K):
            _row_copy(u_ref, j, xb_out, pos_ref[k, j], sems.at[0]).start(priority=k)
        return c

    lax.fori_loop(0, td, issue, 0, unroll=DMA_UNROLL)

    def drain(j, c):
        for k in range(TOP_K):
            _row_copy(u_ref, j, xb_out, pos_ref[k, j], sems.at[0]).wait()
        return c

    lax.fori_loop(0, td, drain, 0, unroll=DMA_UNROLL)


def _dispatch(blk_e, n_used, pos, u2, n_rows, td):
    t_rows = u2.shape[0] // RT
    nstep = t_rows // td
    return pl.pallas_call(
        _dispatch_body,
        out_shape=jax.ShapeDtypeStruct((n_rows * RT, LANES), F32),
        grid_spec=pltpu.PrefetchScalarGridSpec(
            num_scalar_prefetch=2,
            grid=(nstep,),
            in_specs=[
                pl.BlockSpec((SUBLANES, td), lambda i, be, nu: (0, i), memory_space=pltpu.SMEM),
                pl.BlockSpec((td * RT, LANES), lambda i, be, nu: (i, 0)),
            ],
            out_specs=pl.BlockSpec(memory_space=pl.ANY),
            scratch_shapes=[pltpu.VMEM((BM_MOE * RT, LANES), F32), pltpu.SemaphoreType.DMA((2,))],
        ),
        compiler_params=_cparams(("arbitrary",)),
        name="moe_dispatch",
    )(blk_e, n_used, pos, u2)


def _ffn_body(be_ref, nu_ref, x_ref, w1_ref, w3_ref, w2_ref, o_ref, w13_s, w2_s):
    i = pl.program_id(0)
    prev = be_ref[jnp.maximum(i - 1, 0)]
    fresh = (i == 0) | (be_ref[i] != prev)

    @pl.when(fresh & (i < nu_ref[0]))
    def _():
        w13_s[:, :D_EXP] = w1_ref[...].astype(BF16)
        w13_s[:, D_EXP:] = w3_ref[...].astype(BF16)
        w2_s[...] = w2_ref[...].astype(BF16)

    @pl.when(i < nu_ref[0])
    def _():
        x = _from_row_tiles(x_ref, BM_MOE).astype(BF16)
        a = jnp.dot(x, w13_s[...], preferred_element_type=F32)
        hid = _silu(a[:, :D_EXP]) * a[:, D_EXP:]
        _to_row_tiles(o_ref, jnp.dot(hid.astype(BF16), w2_s[...], preferred_element_type=F32))

    @pl.when(i >= nu_ref[0])
    def _():
        o_ref[...] = jnp.zeros_like(o_ref)


def _ffn(blk_e, n_used, xb, w1, w3, w2, layer):
    n_rows = xb.shape[0] // RT
    nb = n_rows // BM_MOE

    def last_used(i, nu):
        return jnp.maximum(jnp.minimum(i, nu[0] - 1), 0)

    def xmap(i, be, nu):
        return (last_used(i, nu), 0)

    def wmap(i, be, nu):
        return (layer, be[last_used(i, nu)], 0, 0)

    return pl.pallas_call(
        _ffn_body,
        out_shape=jax.ShapeDtypeStruct((n_rows * RT, LANES), F32),
        grid_spec=pltpu.PrefetchScalarGridSpec(
            num_scalar_prefetch=2,
            grid=(nb,),
            in_specs=[
                pl.BlockSpec((BM_MOE * RT, LANES), xmap),
                pl.BlockSpec((None, None, D, D_EXP), wmap),
                pl.BlockSpec((None, None, D, D_EXP), wmap),
                pl.BlockSpec((None, None, D_EXP, D), wmap),
            ],
            out_specs=pl.BlockSpec((BM_MOE * RT, LANES), lambda i, be, nu: (i, 0)),
            scratch_shapes=[pltpu.VMEM((D, 2 * D_EXP), BF16), pltpu.VMEM((D_EXP, D), BF16)],
        ),
        compiler_params=_cparams(("arbitrary",)),
        name="moe_ffn",
    )(blk_e, n_used, xb, w1, w3, w2)


def _combine_body(final, pos_ref, nxt_ref, yb_hbm, h_ref, info_ref, mod_ref, fn_ref, o_ref, ybuf, sems):
    i = pl.program_id(0)
    n = pl.num_programs(0)
    td = h_ref.shape[0]
    slot = i % 2

    def gather(p_ref, s, wait):
        def body(j, c):
            for k in range(TOP_K):
                cp = _row_copy(yb_hbm, p_ref[k, j], ybuf.at[s, k], j, sems.at[s])
                cp.wait() if wait else cp.start(priority=k)
            return c

        lax.fori_loop(0, td, body, 0, unroll=DMA_UNROLL)

    @pl.when(i == 0)
    def _():
        gather(pos_ref, slot, False)

    @pl.when(i + 1 < n)
    def _():
        gather(nxt_ref, 1 - slot, False)

    gather(pos_ref, slot, True)
    info = info_ref[...]
    y = (info[:, 4:5] * _from_row_tiles(ybuf.at[slot, 0], td)
         + info[:, 5:6] * _from_row_tiles(ybuf.at[slot, 1], td))
    h2 = h_ref[...] + mod_ref[5:6, :] * y
    if final:
        h2 = h2 * lax.rsqrt(jnp.mean(h2 * h2, axis=-1, keepdims=True) + EPS) * fn_ref[...]
    o_ref[...] = h2


def _combine(pos, yb, h1, info, mod, fnorm, seq, td, final):
    t_rows = h1.shape[0]
    per_b = seq // td
    nstep = t_rows // td
    row = lambda i: (i, 0)
    return pl.pallas_call(
        functools.partial(_combine_body, final),
        out_shape=jax.ShapeDtypeStruct((t_rows, D), F32),
        grid=(nstep,),
        in_specs=[
            pl.BlockSpec((SUBLANES, td), lambda i: (0, i), memory_space=pltpu.SMEM),
            pl.BlockSpec((SUBLANES, td), lambda i: (0, jnp.minimum(i + 1, nstep - 1)),
                         memory_space=pltpu.SMEM),
            pl.BlockSpec(memory_space=pl.ANY),
            pl.BlockSpec((td, D), row),
            pl.BlockSpec((td, LANES), row),
            pl.BlockSpec((None, 6, D), lambda i: (i // per_b, 0, 0)),
            pl.BlockSpec((1, D), lambda i: (0, 0)),
        ],
        out_specs=pl.BlockSpec((td, D), row),
        scratch_shapes=[pltpu.VMEM((2, TOP_K, td * RT, LANES), F32), pltpu.SemaphoreType.DMA((2,))],
        compiler_params=_cparams(("arbitrary",)),
        name="moe_combine",
    )(pos, pos, yb, h1, info, mod, fnorm.reshape(1, D))


def _moe(u2, info, counts, h1, mod, w1, w3, w2, layer, fnorm, seq, final):
    t_rows = h1.shape[0]
    td = min(TD_MOE, seq)
    nb = (t_rows * TOP_K) // BM_MOE + N_EXP
    n_rows = nb * BM_MOE
    cnt = counts[0, :N_EXP].astype(I32)
    padded = (cnt + BM_MOE - 1) // BM_MOE * BM_MOE
    end_p = jnp.cumsum(padded)
    start_p = end_p - padded
    sp_row = jnp.zeros((1, LANES), F32).at[0, :N_EXP].set(start_p.astype(F32))
    blk_e = jnp.minimum(jnp.sum(jnp.arange(nb, dtype=I32)[:, None] * BM_MOE >= end_p[None, :], axis=1),
                        N_EXP - 1).astype(I32)
    n_used = (end_p[-1:] // BM_MOE).astype(I32)
    pos = _moe_pos(info, sp_row)
    xb = _dispatch(blk_e, n_used, pos, u2, n_rows, td)
    yb = _ffn(blk_e, n_used, xb, w1, w3, w2, layer)
    return _combine(pos, yb, h1, info, mod, fnorm, seq, td, final)


def _pad_cols(w, n):
    return jnp.pad(w, ((0, 0), (0, n - w.shape[1])))


def _pad_heads(w):
    w = w.reshape(w.shape[:-1] + (NH_B, DK_B))
    w = jnp.pad(w, [(0, 0)] * (w.ndim - 1) + [(0, DKP_B - DK_B)])
    return w.reshape(w.shape[:-2] + (KP_B,))


def _block_diag(w):
    nh, hw, _ = w.shape
    eye = jnp.eye(nh, dtype=w.dtype)
    return (eye[:, None, :, None] * w[:, :, None, :]).reshape(nh * hw, nh * hw)


def _router_weights(w_grp, b_grp, w_rt, b_rt):
    w = _pad_cols(jnp.concatenate([w_rt, w_grp], axis=1), LANES)
    b = _pad_cols(jnp.concatenate([b_rt, b_grp]).reshape(1, -1), LANES)
    w_hi = w.astype(BF16)
    w_lo = (w - w_hi.astype(F32)).astype(BF16)
    return w_hi, w_lo, b


def kernel(x, c, norm1, norm2, w_ada, b_ada, w_in_ab, conv_a_w, conv_a_b, rg_wa, rg_ba, rg_wx, rg_bx, rg_lam, gla_wg2, gla_bg2, gla_norm, w_out_ab, w_in_c, conv_c_w, dn_a_log, dn_dt_bias, dn_norm, w_out_c, moe_w_grp, moe_b_grp, moe_w_rt, moe_b_rt, moe_w1, moe_w3, moe_w2, final_norm):
    bsz, seq, _ = x.shape
    depth = w_ada.shape[0]
    t_rows = bsz * seq
    mod_all = _ada_mod(c, w_ada, b_ada)
    h = x.reshape(t_rows, D)
    for layer in range(depth):
        mod = mod_all[layer]
        j = layer // 2
        if layer % 2 == 0:
            wi = w_in_ab[j]
            o_q, o_k, o_v, o_gl = 2 * W_A, 2 * W_A + K_B, 2 * W_A + 2 * K_B, 2 * W_A + 2 * K_B + 2 * V_B
            w_in = jnp.concatenate([
                wi[:, :o_q], _pad_heads(wi[:, o_q:o_k]), _pad_heads(wi[:, o_k:o_v]), wi[:, o_v:o_gl],
                _pad_cols(wi[:, o_gl:], LANES)], axis=1).astype(BF16)
            xa, ga, q, k, v, og, gl = _inproj(h, mod, norm1[layer], w_in,
                                              (W_A, W_A, KP_B, KP_B, V_B, V_B, LANES), seq)
            wg_bd = jnp.concatenate([_block_diag(rg_wa[j]), _block_diag(rg_wx[j])], axis=1).astype(BF16)
            bg = jnp.concatenate([rg_ba[j], rg_bx[j]])
            ya = _rglru(xa, ga, conv_a_w[j], conv_a_b[j], wg_bd, bg, rg_lam[j], bsz, seq)
            wg2p = jnp.pad(_pad_heads(gla_wg2[j]), ((0, LANES - R_GATE), (0, 0)))
            bg2p = _pad_heads(gla_bg2[j].reshape(1, K_B))
            ob = _gla(q, k, v, og, gl, wg2p, bg2p, gla_norm[j], bsz, seq)
            ys = (ya, ob)
            ws = (w_out_ab[j][:W_A].astype(BF16), w_out_ab[j][W_A:].astype(BF16))
        else:
            p_c = w_in_c.shape[2]
            w_in = _pad_cols(w_in_c[j], p_c - 2 * NH_C + LANES).astype(BF16)
            q, k, v, z, ba = _inproj(h, mod, norm1[layer], w_in, (W_C, W_C, W_C, W_C, LANES), seq)
            cw = conv_c_w[j]
            lanes_c = jnp.zeros((1, LANES), F32)
            alog = lanes_c.at[0, NH_C:2 * NH_C].set(dn_a_log[j])
            dtb = lanes_c.at[0, NH_C:2 * NH_C].set(dn_dt_bias[j])
            o = _delta(q, k, v, z, ba, cw[:, :W_C], cw[:, W_C:2 * W_C], cw[:, 2 * W_C:], alog, dtb,
                       dn_norm[j].reshape(1, DH_C), bsz, seq)
            ys = (o,)
            ws = (w_out_c[j].astype(BF16),)
        rw = _router_weights(moe_w_grp[layer], moe_b_grp[layer], moe_w_rt[layer], moe_b_rt[layer])
        h1, u2, info, counts = _outproj_router(ys, ws, h, mod, norm2[layer], *rw, seq)
        h = _moe(u2, info, counts, h1, mod, moe_w1, moe_w3, moe_w2, layer, final_norm, seq,
                 final=layer == depth - 1)
    return h.reshape(bsz, seq, D)
```

```python
import functools

import jax
import jax.numpy as jnp
from jax import lax
from jax.experimental import pallas as pl
from jax.experimental.pallas import tpu as pltpu

F32 = jnp.float32
BF16 = jnp.bfloat16
I32 = jnp.int32
HI = lax.Precision.HIGHEST

D = 1024
CHUNK = 64
SUB = 16
CONV_W = 4
EPS = 1e-6
LANES = 128
SUBLANES = 8

W_A = D // 2
NH_A = 8
HW_A = W_A // NH_A
RG_C = 8.0
NH_B = 4
DK_B = 64
DV_B = 128
K_B = NH_B * DK_B
V_B = NH_B * DV_B
DKP_B = LANES
KP_B = NH_B * DKP_B
R_GATE = 16
GATE_NORM = 16.0

NH_C = 8
DH_C = 128
W_C = NH_C * DH_C

N_GROUPS = 4
EPG = 8
N_EXP = N_GROUPS * EPG
TOP_K = 2
D_EXP = 512

VMEM_LIMIT = 56 * 1024 * 1024

TM_PROJ = 512
TS_ROUTER = 512
TT_SEQ = 256
BM_MOE = 512
TD_MOE = 256
TP_MOE = 2048
DMA_UNROLL = 8
DELTA_GROUP = 4


def _cparams(sem):
    return pltpu.CompilerParams(dimension_semantics=sem, vmem_limit_bytes=VMEM_LIMIT)


def _softplus(x):
    return jnp.maximum(x, 0.0) + jnp.log1p(jnp.exp(-jnp.abs(x)))


def _sigmoid(x):
    return jax.nn.sigmoid(x)


def _silu(x):
    return x * jax.nn.sigmoid(x)


def _rms_mod(h, g, sc, sh):
    y = h * lax.rsqrt(jnp.mean(h * h, axis=-1, keepdims=True) + EPS)
    return y * g * (1.0 + sc) + sh


def _dot(a, b):
    return jnp.dot(a.astype(BF16), b.astype(BF16), preferred_element_type=F32)


def _dot_nt(a, b):
    return lax.dot_general(a.astype(BF16), b.astype(BF16), (((1,), (1,)), ((), ())),
                           preferred_element_type=F32)


def _dot_tn(a, b):
    return lax.dot_general(a.astype(BF16), b.astype(BF16), (((0,), (0,)), ((), ())),
                           preferred_element_type=F32)


def _dot_hi(a, b):
    return jnp.dot(a, b, precision=HI, preferred_element_type=F32)


def _ada_body(c_ref, w_ref, b_ref, o_ref):
    o_ref[...] = _dot_hi(_silu(c_ref[...]), w_ref[...]) + b_ref[...]


def _ada_mod(c, w_ada, b_ada):
    depth, _, n6 = w_ada.shape
    bsz = c.shape[0]
    nt = 1536
    c8 = jnp.zeros((SUBLANES, D), F32).at[:bsz].set(c)
    out = pl.pallas_call(
        _ada_body,
        out_shape=jax.ShapeDtypeStruct((depth, SUBLANES, n6), F32),
        grid=(depth, n6 // nt),
        in_specs=[
            pl.BlockSpec((SUBLANES, D), lambda l, j: (0, 0)),
            pl.BlockSpec((None, D, nt), lambda l, j: (l, 0, j)),
            pl.BlockSpec((None, 1, nt), lambda l, j: (l, 0, j)),
        ],
        out_specs=pl.BlockSpec((None, SUBLANES, nt), lambda l, j: (l, 0, j)),
        compiler_params=_cparams(("arbitrary", "arbitrary")),
        name="ada_mod",
    )(c8, w_ada, b_ada.reshape(depth, 1, n6))
    return out[:, :bsz].reshape(depth, bsz, 6, D)


def _inproj_body(widths, h_ref, mod_ref, g_ref, w_ref, *out_refs):
    u = _rms_mod(h_ref[...], g_ref[...], mod_ref[1:2, :], mod_ref[0:1, :]).astype(BF16)
    off = 0
    for o_ref, wd in zip(out_refs, widths):
        o_ref[...] = jnp.dot(u, w_ref[:, off:off + wd], preferred_element_type=F32)
        off += wd


def _inproj(h, mod, g, w_bf16, widths, seq):
    t_rows = h.shape[0]
    tm = min(TM_PROJ, seq)
    per_b = seq // tm
    p = w_bf16.shape[1]
    assert sum(widths) == p
    return pl.pallas_call(
        functools.partial(_inproj_body, widths),
        out_shape=[jax.ShapeDtypeStruct((t_rows, wd), F32) for wd in widths],
        grid=(t_rows // tm,),
        in_specs=[
            pl.BlockSpec((tm, D), lambda i: (i, 0)),
            pl.BlockSpec((None, 6, D), lambda i: (i // per_b, 0, 0)),
            pl.BlockSpec((1, D), lambda i: (0, 0)),
            pl.BlockSpec((D, p), lambda i: (0, 0)),
        ],
        out_specs=[pl.BlockSpec((tm, wd), lambda i: (i, 0)) for wd in widths],
        compiler_params=_cparams(("arbitrary",)),
        name="inproj",
    )(h, mod, g.reshape(1, D), w_bf16)


def _causal_conv(x_ref, cw_ref, xbuf, first):
    tt = x_ref.shape[0]

    @pl.when(first)
    def _():
        xbuf[0:SUBLANES, :] = jnp.zeros((SUBLANES, xbuf.shape[1]), F32)

    xbuf[SUBLANES:SUBLANES + tt, :] = x_ref[...]
    xe = xbuf[...]
    acc = cw_ref[0:1, :] * xe
    for w in range(1, CONV_W):
        acc = pltpu.roll(acc, 1, 0) + cw_ref[w:w + 1, :] * xe
    xbuf[0:SUBLANES, :] = xbuf[tt:tt + SUBLANES, :]
    return acc[SUBLANES:, :]


def _rglru_body(xa_ref, ga_ref, cw_ref, cb_ref, wg_ref, bg_ref, lam_ref, o_ref, xbuf, hc):
    t = pl.program_id(1)
    tt = xa_ref.shape[0]

    @pl.when(t == 0)
    def _():
        hc[...] = jnp.zeros_like(hc)

    xc = _causal_conv(xa_ref, cw_ref, xbuf, t == 0) + cb_ref[...]
    gates = _dot(xc, wg_ref[...]) + bg_ref[...]
    r = _sigmoid(gates[:, :W_A])
    i = _sigmoid(gates[:, W_A:])
    log_a = -RG_C * r * _softplus(-lam_ref[...])
    a = jnp.exp(log_a)
    b = jnp.sqrt(1.0 - a * a) * (i * xc)
    sub = lax.broadcasted_iota(I32, (tt, W_A), 0) % SUBLANES
    s = 1
    while s < SUBLANES:
        a_sh = pltpu.roll(a, s, 0)
        b_sh = pltpu.roll(b, s, 0)
        valid = sub >= s
        b = jnp.where(valid, a * b_sh + b, b)
        a = jnp.where(valid, a * a_sh, a)
        s *= 2
    carry = hc[0:1, :]
    groups = []
    for grp in range(tt // SUBLANES):
        rows = slice(grp * SUBLANES, (grp + 1) * SUBLANES)
        hg = b[rows] + a[rows] * carry
        carry = hg[SUBLANES - 1:SUBLANES, :]
        groups.append(hg)
    h = jnp.concatenate(groups, axis=0)
    hc[...] = jnp.broadcast_to(carry, hc.shape)
    o_ref[...] = jax.nn.gelu(ga_ref[...], approximate=True) * h


def _rglru(xa, ga, conv_w, conv_b, wg_bd, bg, lam, bsz, seq):
    tt = min(TT_SEQ, seq)
    per_b = seq // tt
    row = lambda b, t: (b * per_b + t, 0)
    const = lambda b, t: (0, 0)
    return pl.pallas_call(
        _rglru_body,
        out_shape=jax.ShapeDtypeStruct(xa.shape, F32),
        grid=(bsz, per_b),
        in_specs=[
            pl.BlockSpec((tt, W_A), row),
            pl.BlockSpec((tt, W_A), row),
            pl.BlockSpec((CONV_W, W_A), const),
            pl.BlockSpec((1, W_A), const),
            pl.BlockSpec((W_A, 2 * W_A), const),
            pl.BlockSpec((1, 2 * W_A), const),
            pl.BlockSpec((1, W_A), const),
        ],
        out_specs=pl.BlockSpec((tt, W_A), row),
        scratch_shapes=[pltpu.VMEM((tt + SUBLANES, W_A), F32), pltpu.VMEM((SUBLANES, W_A), F32)],
        compiler_params=_cparams(("arbitrary", "arbitrary")),
        name="rglru",
    )(xa, ga, conv_w, conv_b.reshape(1, W_A), wg_bd, bg.reshape(1, 2 * W_A), lam.reshape(1, W_A))


def _gla_body(q_ref, k_ref, v_ref, og_ref, gl_ref, wg_ref, bg_ref, gn_ref, o_ref, st_ref):
    t = pl.program_id(0)
    nbat, tt = q_ref.shape[0], q_ref.shape[1]

    @pl.when(t == 0)
    def _():
        st_ref[...] = jnp.zeros_like(st_ref)

    ri = lax.broadcasted_iota(I32, (CHUNK, CHUNK), 0)
    ci = lax.broadcasted_iota(I32, (CHUNK, CHUNK), 1)
    tri_incl = (ri >= ci).astype(F32)

    def chunk(c, carry):
        r0 = pl.multiple_of(c * CHUNK, CHUNK)
        rows = pl.ds(r0, CHUNK)
        heads = range(nbat * NH_B)
        blocks = [(b * SUB, (b + 1) * SUB) for b in range(CHUNK // SUB)]
        qh, kh, gh, egh, vh, ogh = [], [], [], [], [], []
        for b in range(nbat):
            x = _dot_hi(gl_ref[b, rows, :], wg_ref[...]) + bg_ref[...]
            lg = -_softplus(-x) * (1.0 / GATE_NORM)
            g = _dot_hi(tri_incl, lg)
            eg = jnp.exp(g)
            q_all = q_ref[b, rows, :] * (DK_B ** -0.5)
            k_all = k_ref[b, rows, :]
            v_all = v_ref[b, rows, :]
            og_all = og_ref[b, rows, :]
            for hh in range(NH_B):
                ks_ = slice(hh * DKP_B, (hh + 1) * DKP_B)
                vs_ = slice(hh * DV_B, (hh + 1) * DV_B)
                qh.append(q_all[:, ks_])
                kh.append(k_all[:, ks_])
                gh.append(g[:, ks_])
                egh.append(eg[:, ks_])
                vh.append(v_all[:, vs_])
                ogh.append(og_all[:, vs_])
        st = [st_ref[h] for h in heads]
        g_last = [gh[h][CHUNK - 1:CHUNK, :] for h in heads]
        att = [[None] * len(blocks) for _ in heads]
        for h in heads:
            for b, (lo, hi) in enumerate(blocks):
                ref_g = gh[h][lo:lo + 1, :]
                qb = qh[h][lo:hi] * jnp.exp(gh[h][lo:hi] - ref_g)
                kb = kh[h][:hi] * jnp.exp(ref_g - gh[h][:hi])
                causal = (lax.broadcasted_iota(I32, (SUB, hi), 1)
                          <= lax.broadcasted_iota(I32, (SUB, hi), 0) + lo)
                att[h][b] = jnp.where(causal, _dot_nt(qb, kb), 0.0)
        o_inter = [_dot_nt(qh[h] * egh[h], st[h]) for h in heads]
        upd = [_dot_tn(vh[h], kh[h] * jnp.exp(g_last[h] - gh[h])) for h in heads]
        o_intra = [[_dot(att[h][b], vh[h][:hi]) for b, (lo, hi) in enumerate(blocks)] for h in heads]
        outs = []
        for h in heads:
            st_ref[h] = st[h] * jnp.exp(g_last[h]) + upd[h]
            o = o_inter[h] + jnp.concatenate(o_intra[h], axis=0)
            o = o * lax.rsqrt(jnp.mean(o * o, axis=-1, keepdims=True) + EPS) * gn_ref[...]
            outs.append(o * _silu(ogh[h]))
        for b in range(nbat):
            o_ref[b, rows, :] = jnp.concatenate(outs[b * NH_B:(b + 1) * NH_B], axis=1)
        return carry

    lax.fori_loop(0, tt // CHUNK, chunk, 0)


def _gla(q, k, v, og, gl, wg2p, bg2p, gnorm, bsz, seq):
    tt = min(TT_SEQ, seq)
    row = lambda t: (0, t, 0)
    const = lambda t: (0, 0)
    r3 = lambda a: a.reshape(bsz, seq, a.shape[-1])
    out = pl.pallas_call(
        _gla_body,
        out_shape=jax.ShapeDtypeStruct((bsz, seq, V_B), F32),
        grid=(seq // tt,),
        in_specs=[
            pl.BlockSpec((bsz, tt, KP_B), row),
            pl.BlockSpec((bsz, tt, KP_B), row),
            pl.BlockSpec((bsz, tt, V_B), row),
            pl.BlockSpec((bsz, tt, V_B), row),
            pl.BlockSpec((bsz, tt, LANES), row),
            pl.BlockSpec((LANES, KP_B), const),
            pl.BlockSpec((1, KP_B), const),
            pl.BlockSpec((1, DV_B), const),
        ],
        out_specs=pl.BlockSpec((bsz, tt, V_B), row),
        scratch_shapes=[pltpu.VMEM((bsz * NH_B, DV_B, DKP_B), F32)],
        compiler_params=_cparams(("arbitrary",)),
        name="gla",
    )(r3(q), r3(k), r3(v), r3(og), r3(gl), wg2p, bg2p, gnorm.reshape(1, DV_B))
    return out.reshape(bsz * seq, V_B)


def _unit_lower_solve(lows, rhss, ri, ci):
    n = range(len(lows))
    same = (ri // SUB) == (ci // SUB)
    eye = (ri == ci).astype(F32)
    dg = [jnp.where(same, lows[h], 0.0) for h in n]
    off = [lows[h] - dg[h] for h in n]
    tinv = [eye - dg[h] for h in n]
    pw = dg
    e = 2
    while e < SUB:
        pw = [_dot(pw[h], pw[h]) for h in n]
        tinv = [tinv[h] + _dot(tinv[h], pw[h]) for h in n]
        e *= 2
    nmat = [_dot(tinv[h], off[h]) for h in n]
    x = [_dot(tinv[h], rhss[h]) for h in n]
    sol = [[x[h][0:SUB]] for h in n]
    for blk in range(1, CHUNK // SUB):
        lo, hi = blk * SUB, (blk + 1) * SUB
        for h in n:
            prev = jnp.concatenate(sol[h], axis=0)
            sol[h].append(x[h][lo:hi] - _dot(nmat[h][lo:hi, :lo], prev))
    return [jnp.concatenate(sol[h], axis=0) for h in n]


def _delta_body(q_ref, k_ref, v_ref, z_ref, ba_ref, cq_ref, ck_ref, cv_ref, alog_ref, dtb_ref, dn_ref,
                o_ref, qbuf, kbuf, vbuf, qs, ks, vs, st_ref):
    t = pl.program_id(0)
    nbat, tt = q_ref.shape[0], q_ref.shape[1]

    @pl.when(t == 0)
    def _():
        st_ref[...] = jnp.zeros_like(st_ref)

    for b in range(nbat):
        qs[b] = _silu(_causal_conv(q_ref.at[b], cq_ref, qbuf.at[b], t == 0))
        ks[b] = _silu(_causal_conv(k_ref.at[b], ck_ref, kbuf.at[b], t == 0))
        vs[b] = _silu(_causal_conv(v_ref.at[b], cv_ref, vbuf.at[b], t == 0))

    ri = lax.broadcasted_iota(I32, (CHUNK, CHUNK), 0)
    ci = lax.broadcasted_iota(I32, (CHUNK, CHUNK), 1)
    tri_incl = (ri >= ci).astype(F32)
    incl = ri >= ci
    strict = ri > ci

    nseq = nbat * NH_C

    def chunk_group(cg, carry):
        rows_of, seqs = [], []
        qh, kh, vh, zh, beta, gcol, egcol, g_last, decay = [], [], [], [], [], [], [], [], []
        for sub in range(DELTA_GROUP):
            r0 = pl.multiple_of((cg * DELTA_GROUP + sub) * CHUNK, CHUNK)
            rows = pl.ds(r0, CHUNK)
            rows_of.append(rows)
            for b in range(nbat):
                ba = ba_ref[b, rows, :]
                beta_all = _sigmoid(ba)
                g_all = -jnp.exp(alog_ref[...]) * _softplus(ba + dtb_ref[...])
                gcum = _dot_hi(tri_incl, g_all)
                gcum_t = gcum.T
                egc = jnp.exp(gcum)
                q_all, k_all, v_all, z_all = qs[b, rows, :], ks[b, rows, :], vs[b, rows, :], z_ref[b, rows, :]
                for hh in range(NH_C):
                    hs = slice(hh * DH_C, (hh + 1) * DH_C)
                    q_, k_ = q_all[:, hs], k_all[:, hs]
                    qh.append(q_ * lax.rsqrt(jnp.sum(q_ * q_, axis=-1, keepdims=True) + EPS) * (DH_C ** -0.5))
                    kh.append(k_ * lax.rsqrt(jnp.sum(k_ * k_, axis=-1, keepdims=True) + EPS))
                    vh.append(v_all[:, hs])
                    zh.append(z_all[:, hs])
                    beta.append(beta_all[:, hh:hh + 1])
                    gcol.append(gcum[:, NH_C + hh:NH_C + hh + 1])
                    egcol.append(egc[:, NH_C + hh:NH_C + hh + 1])
                    g_last.append(gcum[CHUNK - 1:CHUNK, NH_C + hh:NH_C + hh + 1])
                    grow = gcum_t[NH_C + hh:NH_C + hh + 1, :]
                    decay.append(jnp.exp(jnp.where(incl, gcol[-1] - grow, -jnp.inf)))
        every = range(DELTA_GROUP * nseq)
        kbeta = [kh[h] * beta[h] for h in every]
        kq = [_dot_nt(jnp.concatenate([kbeta[h], qh[h]], axis=0), kh[h]) for h in every]
        low = [jnp.where(strict, kq[h][:CHUNK] * decay[h], 0.0) for h in every]
        rhs = [jnp.concatenate([vh[h] * beta[h], kbeta[h] * egcol[h]], axis=1) for h in every]
        sol = _unit_lower_solve(low, rhs, ri, ci)
        st = [st_ref[s] for s in range(nseq)]
        for sub in range(DELTA_GROUP):
            heads = range(sub * nseq, (sub + 1) * nseq)
            ws = {h: _dot(jnp.concatenate([sol[h][:, DH_C:], qh[h] * egcol[h]], axis=0), st[h - sub * nseq])
                  for h in heads}
            v_new = {h: sol[h][:, :DH_C] - ws[h][:CHUNK] for h in heads}
            o_intra = {h: _dot(kq[h][CHUNK:] * decay[h], v_new[h]) for h in heads}
            upd = {h: _dot_tn(kh[h] * jnp.exp(g_last[h] - gcol[h]), v_new[h]) for h in heads}
            outs = []
            for h in heads:
                st[h - sub * nseq] = st[h - sub * nseq] * jnp.exp(g_last[h]) + upd[h]
                o = ws[h][CHUNK:] + o_intra[h]
                o = o * lax.rsqrt(jnp.mean(o * o, axis=-1, keepdims=True) + EPS) * dn_ref[...]
                outs.append(o * _silu(zh[h]))
            for b in range(nbat):
                o_ref[b, rows_of[sub], :] = jnp.concatenate(outs[b * NH_C:(b + 1) * NH_C], axis=1)
        for s in range(nseq):
            st_ref[s] = st[s]
        return carry

    lax.fori_loop(0, tt // (CHUNK * DELTA_GROUP), chunk_group, 0)


def _delta(q, k, v, z, ba, cq, ck, cv, alog, dtb, dnorm, bsz, seq):
    tt = min(TT_SEQ, seq)
    row = lambda t: (0, t, 0)
    const = lambda t: (0, 0)
    big = pl.BlockSpec((bsz, tt, W_C), row)
    cw = pl.BlockSpec((CONV_W, W_C), const)
    vec = pl.BlockSpec((1, LANES), const)
    r3 = lambda a: a.reshape(bsz, seq, a.shape[-1])
    out = pl.pallas_call(
        _delta_body,
        out_shape=jax.ShapeDtypeStruct((bsz, seq, W_C), F32),
        grid=(seq // tt,),
        in_specs=[big, big, big, big, pl.BlockSpec((bsz, tt, LANES), row), cw, cw, cw, vec, vec, vec],
        out_specs=big,
        scratch_shapes=[pltpu.VMEM((bsz, tt + SUBLANES, W_C), F32)] * 3 + [pltpu.VMEM((bsz, tt, W_C), F32)] * 3
        + [pltpu.VMEM((bsz * NH_C, DH_C, DH_C), F32)],
        compiler_params=_cparams(("arbitrary",)),
        name="delta",
    )(r3(q), r3(k), r3(v), r3(z), r3(ba), cq, ck, cv, alog, dtb, dnorm)
    return out.reshape(bsz * seq, W_C)


def _outproj_router_body(n_parts, *refs):
    y_refs = refs[:n_parts]
    w_refs = refs[n_parts:2 * n_parts]
    (h_ref, mod_ref, g2_ref, wr_hi_ref, wr_lo_ref, br_ref,
     h1_ref, u2_ref, info_ref, cnt_ref, carry) = refs[2 * n_parts:]
    i = pl.program_id(0)
    tm = h_ref.shape[0]

    @pl.when(i == 0)
    def _():
        carry[...] = jnp.zeros_like(carry)

    ts = min(TS_ROUTER, tm)
    count = carry[0:1, :]
    for r0 in range(0, tm, ts):
        count = _route_rows(slice(r0, r0 + ts), count, y_refs, w_refs, h_ref, mod_ref, g2_ref, wr_hi_ref,
                            wr_lo_ref, br_ref, h1_ref, u2_ref, info_ref)
    carry[...] = jnp.broadcast_to(count, carry.shape)
    cnt_ref[...] = carry[...]


def _route_rows(rows, count, y_refs, w_refs, h_ref, mod_ref, g2_ref, wr_hi_ref, wr_lo_ref, br_ref,
                h1_ref, u2_ref, info_ref):
    tm = rows.stop - rows.start
    m = None
    for y_ref, w_ref in zip(y_refs, w_refs):
        part = _dot(y_ref[rows, :], w_ref[...])
        m = part if m is None else m + part
    h1 = h_ref[rows, :] + mod_ref[2:3, :] * m
    h1_ref[rows, :] = h1
    u2 = _rms_mod(h1, g2_ref[...], mod_ref[4:5, :], mod_ref[3:4, :])
    _to_row_tiles(u2_ref, u2, rows.start)

    lane = lax.broadcasted_iota(I32, (tm, LANES), 1)
    neg = -jnp.inf
    u_hi = u2.astype(BF16)
    u_lo = (u2 - u_hi.astype(F32)).astype(BF16)
    w_hi = wr_hi_ref[...]
    logits = (jnp.dot(u_hi, w_hi, preferred_element_type=F32)
              + jnp.dot(u_hi, wr_lo_ref[...], preferred_element_type=F32)
              + jnp.dot(u_lo, w_hi, preferred_element_type=F32)) + br_ref[...]
    lgp = jnp.where((lane >= N_EXP) & (lane < N_EXP + N_GROUPS), logits, neg)
    gmax = jnp.max(lgp, axis=-1, keepdims=True)
    pg_top = 1.0 / jnp.sum(jnp.exp(lgp - gmax), axis=-1, keepdims=True)
    g_idx = jnp.min(jnp.where(lgp == gmax, lane, LANES), axis=-1, keepdims=True) - N_EXP
    in_group = (lane // EPG == g_idx) & (lane < N_EXP)
    le = jnp.where(in_group, logits, neg)
    emax = jnp.max(le, axis=-1, keepdims=True)
    pe = jnp.exp(le - emax)
    pe = jnp.where(in_group, pe / jnp.sum(pe, axis=-1, keepdims=True), -1.0)
    p0 = jnp.max(pe, axis=-1, keepdims=True)
    e0 = jnp.min(jnp.where(pe == p0, lane, LANES), axis=-1, keepdims=True)
    pe1 = jnp.where(lane == e0, -1.0, pe)
    p1 = jnp.max(pe1, axis=-1, keepdims=True)
    e1 = jnp.min(jnp.where(pe1 == p1, lane, LANES), axis=-1, keepdims=True)
    den = p0 + p1
    w0 = pg_top * p0 / den
    w1 = pg_top * p1 / den
    hot = ((lane == e0) | (lane == e1)).astype(F32)
    ri = lax.broadcasted_iota(I32, (tm, tm), 0)
    ci = lax.broadcasted_iota(I32, (tm, tm), 1)
    before = _dot((ri > ci).astype(F32), hot) + count
    r0 = jnp.sum(jnp.where(lane == e0, before, 0.0), axis=-1, keepdims=True)
    r1 = jnp.sum(jnp.where(lane == e1, before, 0.0), axis=-1, keepdims=True)
    info = jnp.where(lane == 0, e0.astype(F32), 0.0)
    info = jnp.where(lane == 1, e1.astype(F32), info)
    info = jnp.where(lane == 2, r0, info)
    info = jnp.where(lane == 3, r1, info)
    info = jnp.where(lane == 4, w0, info)
    info = jnp.where(lane == 5, w1, info)
    info_ref[rows, :] = info
    return count + jnp.sum(hot, axis=0, keepdims=True)


def _outproj_router(ys, ws, h, mod, g2, wr_hi, wr_lo, br, seq):
    t_rows = h.shape[0]
    tm = min(TM_PROJ, seq)
    per_b = seq // tm
    n = len(ys)
    row = lambda i: (i, 0)
    const = lambda i: (0, 0)
    in_specs = [pl.BlockSpec((tm, y.shape[1]), row) for y in ys]
    in_specs += [pl.BlockSpec(w.shape, const) for w in ws]
    in_specs += [
        pl.BlockSpec((tm, D), row),
        pl.BlockSpec((None, 6, D), lambda i: (i // per_b, 0, 0)),
        pl.BlockSpec((1, D), const),
        pl.BlockSpec((D, LANES), const),
        pl.BlockSpec((D, LANES), const),
        pl.BlockSpec((1, LANES), const),
    ]
    return pl.pallas_call(
        functools.partial(_outproj_router_body, n),
        out_shape=[
            jax.ShapeDtypeStruct((t_rows, D), F32),
            jax.ShapeDtypeStruct((t_rows * RT, LANES), F32),
            jax.ShapeDtypeStruct((t_rows, LANES), F32),
            jax.ShapeDtypeStruct((SUBLANES, LANES), F32),
        ],
        grid=(t_rows // tm,),
        in_specs=in_specs,
        out_specs=[
            pl.BlockSpec((tm, D), row),
            pl.BlockSpec((tm * RT, LANES), row),
            pl.BlockSpec((tm, LANES), row),
            pl.BlockSpec((SUBLANES, LANES), const),
        ],
        scratch_shapes=[pltpu.VMEM((SUBLANES, LANES), F32)],
        compiler_params=_cparams(("arbitrary",)),
        name="outproj_router",
    )(*ys, *ws, h, mod, g2.reshape(1, D), wr_hi, wr_lo, br)


RT = D // LANES
assert RT == SUBLANES


def _to_row_tiles(ref, x, row0=0):
    n = x.shape[0]
    for c in range(RT):
        ref[pl.ds(row0 * RT + c, n, stride=RT), :] = x[:, c * LANES:(c + 1) * LANES]


def _from_row_tiles(ref, n):
    return jnp.concatenate([ref[pl.ds(c, n, stride=RT), :] for c in range(RT)], axis=1)


def _row_copy(src, src_row, dst, dst_row, sem):
    return pltpu.make_async_copy(src.at[pl.ds(pl.multiple_of(src_row * RT, RT), RT)],
                                 dst.at[pl.ds(pl.multiple_of(dst_row * RT, RT), RT)], sem)


def _pos_body(info_ref, sp_ref, o_ref):
    info = info_ref[...]
    td = info.shape[0]
    lane = lax.broadcasted_iota(I32, (td, LANES), 1)
    sp = sp_ref[...]
    packed = jnp.zeros((td, LANES), F32)
    for k in range(TOP_K):
        e = info[:, k:k + 1].astype(I32)
        start = jnp.sum(jnp.where(lane == e, sp, 0.0), axis=-1, keepdims=True)
        packed = jnp.where(lane == k, start + info[:, TOP_K + k:TOP_K + k + 1], packed)
    pick = (lax.broadcasted_iota(I32, (SUBLANES, LANES), 0)
            == lax.broadcasted_iota(I32, (SUBLANES, LANES), 1)).astype(F32)
    o_ref[...] = lax.dot_general(pick, packed, (((1,), (1,)), ((), ())), precision=HI,
                                 preferred_element_type=F32).astype(I32)


def _moe_pos(info, sp_row):
    t_rows = info.shape[0]
    tp = min(TP_MOE, t_rows)
    return pl.pallas_call(
        _pos_body,
        out_shape=jax.ShapeDtypeStruct((SUBLANES, t_rows), I32),
        grid=(t_rows // tp,),
        in_specs=[pl.BlockSpec((tp, LANES), lambda i: (i, 0)), pl.BlockSpec((1, LANES), lambda i: (0, 0))],
        out_specs=pl.BlockSpec((SUBLANES, tp), lambda i: (0, i)),
        compiler_params=_cparams(("arbitrary",)),
        name="moe_pos",
    )(info, sp_row)


def _dispatch_body(be_ref, nu_ref, pos_ref, u_ref, xb_out, zero_s, sems):
    i = pl.program_id(0)
    td = u_ref.shape[0] // RT
    nb = be_ref.shape[0]

    @pl.when(i == 0)
    def _():
        zero_s[...] = jnp.zeros_like(zero_s)

        def needs_zero(b):
            return (b >= nu_ref[0] - 1) | (be_ref[jnp.minimum(b + 1, nb - 1)] != be_ref[b])

        def block_copy(b):
            return pltpu.make_async_copy(zero_s, xb_out.at[pl.ds(b * (BM_MOE * RT), BM_MOE * RT)], sems.at[1])

        def zissue(b, c):
            @pl.when(needs_zero(b))
            def _():
                block_copy(b).start()
            return c

        def zdrain(b, c):
            @pl.when(needs_zero(b))
            def _():
                block_copy(b).wait()
            return c

        lax.fori_loop(0, nb, zissue, 0)
        lax.fori_loop(0, nb, zdrain, 0)

    def issue(j, c):
        for k in range(TOP_K):
            _row_copy(u_ref, j, xb_out, pos_ref[k, j], sems.at[0]).start(priority=k)
        return c

    lax.fori_loop(0, td, issue, 0, unroll=DMA_UNROLL)

    def drain(j, c):
        for k in range(TOP_K):
            _row_copy(u_ref, j, xb_out, pos_ref[k, j], sems.at[0]).wait()
        return c

    lax.fori_loop(0, td, drain, 0, unroll=DMA_UNROLL)


def _dispatch(blk_e, n_used, pos, u2, n_rows, td):
    t_rows = u2.shape[0] // RT
    nstep = t_rows // td
    return pl.pallas_call(
        _dispatch_body,
        out_shape=jax.ShapeDtypeStruct((n_rows * RT, LANES), F32),
        grid_spec=pltpu.PrefetchScalarGridSpec(
            num_scalar_prefetch=2,
            grid=(nstep,),
            in_specs=[
                pl.BlockSpec((SUBLANES, td), lambda i, be, nu: (0, i), memory_space=pltpu.SMEM),
                pl.BlockSpec((td * RT, LANES), lambda i, be, nu: (i, 0)),
            ],
            out_specs=pl.BlockSpec(memory_space=pl.ANY),
            scratch_shapes=[pltpu.VMEM((BM_MOE * RT, LANES), F32), pltpu.SemaphoreType.DMA((2,))],
        ),
        compiler_params=_cparams(("arbitrary",)),
        name="moe_dispatch",
    )(blk_e, n_used, pos, u2)


def _ffn_body(be_ref, nu_ref, x_ref, w1_ref, w3_ref, w2_ref, o_ref, w13_s, w2_s):
    i = pl.program_id(0)
    prev = be_ref[jnp.maximum(i - 1, 0)]
    fresh = (i == 0) | (be_ref[i] != prev)

    @pl.when(fresh & (i < nu_ref[0]))
    def _():
        w13_s[:, :D_EXP] = w1_ref[...].astype(BF16)
        w13_s[:, D_EXP:] = w3_ref[...].astype(BF16)
        w2_s[...] = w2_ref[...].astype(BF16)

    @pl.when(i < nu_ref[0])
    def _():
        x = _from_row_tiles(x_ref, BM_MOE).astype(BF16)
        a = jnp.dot(x, w13_s[...], preferred_element_type=F32)
        hid = _silu(a[:, :D_EXP]) * a[:, D_EXP:]
        _to_row_tiles(o_ref, jnp.dot(hid.astype(BF16), w2_s[...], preferred_element_type=F32))

    @pl.when(i >= nu_ref[0])
    def _():
        o_ref[...] = jnp.zeros_like(o_ref)


def _ffn(blk_e, n_used, xb, w1, w3, w2, layer):
    n_rows = xb.shape[0] // RT
    nb = n_rows // BM_MOE

    def last_used(i, nu):
        return jnp.maximum(jnp.minimum(i, nu[0] - 1), 0)

    def xmap(i, be, nu):
        return (last_used(i, nu), 0)

    def wmap(i, be, nu):
        return (layer, be[last_used(i, nu)], 0, 0)

    return pl.pallas_call(
        _ffn_body,
        out_shape=jax.ShapeDtypeStruct((n_rows * RT, LANES), F32),
        grid_spec=pltpu.PrefetchScalarGridSpec(
            num_scalar_prefetch=2,
            grid=(nb,),
            in_specs=[
                pl.BlockSpec((BM_MOE * RT, LANES), xmap),
                pl.BlockSpec((None, None, D, D_EXP), wmap),
                pl.BlockSpec((None, None, D, D_EXP), wmap),
                pl.BlockSpec((None, None, D_EXP, D), wmap),
            ],
            out_specs=pl.BlockSpec((BM_MOE * RT, LANES), lambda i, be, nu: (i, 0)),
            scratch_shapes=[pltpu.VMEM((D, 2 * D_EXP), BF16), pltpu.VMEM((D_EXP, D), BF16)],
        ),
        compiler_params=_cparams(("arbitrary",)),
        name="moe_ffn",
    )(blk_e, n_used, xb, w1, w3, w2)


def _combine_body(final, pos_ref, nxt_ref, yb_hbm, h_ref, info_ref, mod_ref, fn_ref, o_ref, ybuf, sems):
    i = pl.program_id(0)
    n = pl.num_programs(0)
    td = h_ref.shape[0]
    slot = i % 2

    def gather(p_ref, s, wait):
        def body(j, c):
            for k in range(TOP_K):
                cp = _row_copy(yb_hbm, p_ref[k, j], ybuf.at[s, k], j, sems.at[s])
                cp.wait() if wait else cp.start(priority=k)
            return c

        lax.fori_loop(0, td, body, 0, unroll=DMA_UNROLL)

    @pl.when(i == 0)
    def _():
        gather(pos_ref, slot, False)

    @pl.when(i + 1 < n)
    def _():
        gather(nxt_ref, 1 - slot, False)

    gather(pos_ref, slot, True)
    info = info_ref[...]
    y = (info[:, 4:5] * _from_row_tiles(ybuf.at[slot, 0], td)
         + info[:, 5:6] * _from_row_tiles(ybuf.at[slot, 1], td))
    h2 = h_ref[...] + mod_ref[5:6, :] * y
    if final:
        h2 = h2 * lax.rsqrt(jnp.mean(h2 * h2, axis=-1, keepdims=True) + EPS) * fn_ref[...]
    o_ref[...] = h2


def _combine(pos, yb, h1, info, mod, fnorm, seq, td, final):
    t_rows = h1.shape[0]
    per_b = seq // td
    nstep = t_rows // td
    row = lambda i: (i, 0)
    return pl.pallas_call(
        functools.partial(_combine_body, final),
        out_shape=jax.ShapeDtypeStruct((t_rows, D), F32),
        grid=(nstep,),
        in_specs=[
            pl.BlockSpec((SUBLANES, td), lambda i: (0, i), memory_space=pltpu.SMEM),
            pl.BlockSpec((SUBLANES, td), lambda i: (0, jnp.minimum(i + 1, nstep - 1)),
                         memory_space=pltpu.SMEM),
            pl.BlockSpec(memory_space=pl.ANY),
            pl.BlockSpec((td, D), row),
            pl.BlockSpec((td, LANES), row),
            pl.BlockSpec((None, 6, D), lambda i: (i // per_b, 0, 0)),
            pl.BlockSpec((1, D), lambda i: (0, 0)),
        ],
        out_specs=pl.BlockSpec((td, D), row),
        scratch_shapes=[pltpu.VMEM((2, TOP_K, td * RT, LANES), F32), pltpu.SemaphoreType.DMA((2,))],
        compiler_params=_cparams(("arbitrary",)),
        name="moe_combine",
    )(pos, pos, yb, h1, info, mod, fnorm.reshape(1, D))


def _moe(u2, info, counts, h1, mod, w1, w3, w2, layer, fnorm, seq, final):
    t_rows = h1.shape[0]
    td = min(TD_MOE, seq)
    nb = (t_rows * TOP_K) // BM_MOE + N_EXP
    n_rows = nb * BM_MOE
    cnt = counts[0, :N_EXP].astype(I32)
    padded = (cnt + BM_MOE - 1) // BM_MOE * BM_MOE
    end_p = jnp.cumsum(padded)
    start_p = end_p - padded
    sp_row = jnp.zeros((1, LANES), F32).at[0, :N_EXP].set(start_p.astype(F32))
    blk_e = jnp.minimum(jnp.sum(jnp.arange(nb, dtype=I32)[:, None] * BM_MOE >= end_p[None, :], axis=1),
                        N_EXP - 1).astype(I32)
    n_used = (end_p[-1:] // BM_MOE).astype(I32)
    pos = _moe_pos(info, sp_row)
    xb = _dispatch(blk_e, n_used, pos, u2, n_rows, td)
    yb = _ffn(blk_e, n_used, xb, w1, w3, w2, layer)
    return _combine(pos, yb, h1, info, mod, fnorm, seq, td, final)


def _pad_cols(w, n):
    return jnp.pad(w, ((0, 0), (0, n - w.shape[1])))


def _pad_heads(w):
    w = w.reshape(w.shape[:-1] + (NH_B, DK_B))
    w = jnp.pad(w, [(0, 0)] * (w.ndim - 1) + [(0, DKP_B - DK_B)])
    return w.reshape(w.shape[:-2] + (KP_B,))


def _block_diag(w):
    nh, hw, _ = w.shape
    eye = jnp.eye(nh, dtype=w.dtype)
    return (eye[:, None, :, None] * w[:, :, None, :]).reshape(nh * hw, nh * hw)


def _router_weights(w_grp, b_grp, w_rt, b_rt):
    w = _pad_cols(jnp.concatenate([w_rt, w_grp], axis=1), LANES)
    b = _pad_cols(jnp.concatenate([b_rt, b_grp]).reshape(1, -1), LANES)
    w_hi = w.astype(BF16)
    w_lo = (w - w_hi.astype(F32)).astype(BF16)
    return w_hi, w_lo, b


def kernel(x, c, norm1, norm2, w_ada, b_ada, w_in_ab, conv_a_w, conv_a_b, rg_wa, rg_ba, rg_wx, rg_bx, rg_lam, gla_wg2, gla_bg2, gla_norm, w_out_ab, w_in_c, conv_c_w, dn_a_log, dn_dt_bias, dn_norm, w_out_c, moe_w_grp, moe_b_grp, moe_w_rt, moe_b_rt, moe_w1, moe_w3, moe_w2, final_norm):
    bsz, seq, _ = x.shape
    depth = w_ada.shape[0]
    t_rows = bsz * seq
    mod_all = _ada_mod(c, w_ada, b_ada)
    h = x.reshape(t_rows, D)
    for layer in range(depth):
        mod = mod_all[layer]
        j = layer // 2
        if layer % 2 == 0:
            wi = w_in_ab[j]
            o_q, o_k, o_v, o_gl = 2 * W_A, 2 * W_A + K_B, 2 * W_A + 2 * K_B, 2 * W_A + 2 * K_B + 2 * V_B
            w_in = jnp.concatenate([
                wi[:, :o_q], _pad_heads(wi[:, o_q:o_k]), _pad_heads(wi[:, o_k:o_v]), wi[:, o_v:o_gl],
                _pad_cols(wi[:, o_gl:], LANES)], axis=1).astype(BF16)
            xa, ga, q, k, v, og, gl = _inproj(h, mod, norm1[layer], w_in,
                                              (W_A, W_A, KP_B, KP_B, V_B, V_B, LANES), seq)
            wg_bd = jnp.concatenate([_block_diag(rg_wa[j]), _block_diag(rg_wx[j])], axis=1).astype(BF16)
            bg = jnp.concatenate([rg_ba[j], rg_bx[j]])
            ya = _rglru(xa, ga, conv_a_w[j], conv_a_b[j], wg_bd, bg, rg_lam[j], bsz, seq)
            wg2p = jnp.pad(_pad_heads(gla_wg2[j]), ((0, LANES - R_GATE), (0, 0)))
            bg2p = _pad_heads(gla_bg2[j].reshape(1, K_B))
            ob = _gla(q, k, v, og, gl, wg2p, bg2p, gla_norm[j], bsz, seq)
            ys = (ya, ob)
            ws = (w_out_ab[j][:W_A].astype(BF16), w_out_ab[j][W_A:].astype(BF16))
        else:
            p_c = w_in_c.shape[2]
            w_in = _pad_cols(w_in_c[j], p_c - 2 * NH_C + LANES).astype(BF16)
            q, k, v, z, ba = _inproj(h, mod, norm1[layer], w_in, (W_C, W_C, W_C, W_C, LANES), seq)
            cw = conv_c_w[j]
            lanes_c = jnp.zeros((1, LANES), F32)
            alog = lanes_c.at[0, NH_C:2 * NH_C].set(dn_a_log[j])
            dtb = lanes_c.at[0, NH_C:2 * NH_C].set(dn_dt_bias[j])
            o = _delta(q, k, v, z, ba, cw[:, :W_C], cw[:, W_C:2 * W_C], cw[:, 2 * W_C:], alog, dtb,
                       dn_norm[j].reshape(1, DH_C), bsz, seq)
            ys = (o,)
            ws = (w_out_c[j].astype(BF16),)
        rw = _router_weights(moe_w_grp[layer], moe_b_grp[layer], moe_w_rt[layer], moe_b_rt[layer])
        h1, u2, info, counts = _outproj_router(ys, ws, h, mod, norm2[layer], *rw, seq)
        h = _moe(u2, info, counts, h1, mod, moe_w1, moe_w3, moe_w2, layer, final_norm, seq,
                 final=layer == depth - 1)
    return h.reshape(bsz, seq, D)
```

```python
import functools

import jax
import jax.numpy as jnp
from jax import lax
from jax.experimental import pallas as pl
from jax.experimental.pallas import tpu as pltpu

F32 = jnp.float32
BF16 = jnp.bfloat16
I32 = jnp.int32
HI = lax.Precision.HIGHEST

D = 1024
CHUNK = 64
SUB = 16
CONV_W = 4
EPS = 1e-6
LANES = 128
SUBLANES = 8

W_A = D // 2
NH_A = 8
HW_A = W_A // NH_A
RG_C = 8.0
NH_B = 4
DK_B = 64
DV_B = 128
K_B = NH_B * DK_B
V_B = NH_B * DV_B
DKP_B = LANES
KP_B = NH_B * DKP_B
R_GATE = 16
GATE_NORM = 16.0

NH_C = 8
DH_C = 128
W_C = NH_C * DH_C

N_GROUPS = 4
EPG = 8
N_EXP = N_GROUPS * EPG
TOP_K = 2
D_EXP = 512

VMEM_LIMIT = 56 * 1024 * 1024

TM_PROJ = 512
TT_SEQ = 256
BM_MOE = 512
TD_MOE = 256
TP_MOE = 2048
DMA_UNROLL = 8
DELTA_GROUP = 4
GLA_GROUP = 4


def _cparams(sem):
    return pltpu.CompilerParams(dimension_semantics=sem, vmem_limit_bytes=VMEM_LIMIT)


def _softplus(x):
    return jnp.maximum(x, 0.0) + jnp.log1p(jnp.exp(-jnp.abs(x)))


def _sigmoid(x):
    return jax.nn.sigmoid(x)


def _silu(x):
    return x * jax.nn.sigmoid(x)


def _rms_mod(h, g, sc, sh):
    y = h * lax.rsqrt(jnp.mean(h * h, axis=-1, keepdims=True) + EPS)
    return y * g * (1.0 + sc) + sh


def _dot(a, b):
    return jnp.dot(a.astype(BF16), b.astype(BF16), preferred_element_type=F32)


def _dot_nt(a, b):
    return lax.dot_general(a.astype(BF16), b.astype(BF16), (((1,), (1,)), ((), ())),
                           preferred_element_type=F32)


def _dot_tn(a, b):
    return lax.dot_general(a.astype(BF16), b.astype(BF16), (((0,), (0,)), ((), ())),
                           preferred_element_type=F32)


def _dot_hi(a, b):
    return jnp.dot(a, b, precision=HI, preferred_element_type=F32)


def _ada_body(c_ref, w_ref, b_ref, o_ref):
    o_ref[...] = _dot_hi(_silu(c_ref[...]), w_ref[...]) + b_ref[...]


def _ada_mod(c, w_ada, b_ada):
    depth, _, n6 = w_ada.shape
    bsz = c.shape[0]
    nt = 1536
    c8 = jnp.zeros((SUBLANES, D), F32).at[:bsz].set(c)
    out = pl.pallas_call(
        _ada_body,
        out_shape=jax.ShapeDtypeStruct((depth, SUBLANES, n6), F32),
        grid=(depth, n6 // nt),
        in_specs=[
            pl.BlockSpec((SUBLANES, D), lambda l, j: (0, 0)),
            pl.BlockSpec((None, D, nt), lambda l, j: (l, 0, j)),
            pl.BlockSpec((None, 1, nt), lambda l, j: (l, 0, j)),
        ],
        out_specs=pl.BlockSpec((None, SUBLANES, nt), lambda l, j: (l, 0, j)),
        compiler_params=_cparams(("arbitrary", "arbitrary")),
        name="ada_mod",
    )(c8, w_ada, b_ada.reshape(depth, 1, n6))
    return out[:, :bsz].reshape(depth, bsz, 6, D)


def _inproj_body(widths, h_ref, mod_ref, g_ref, w_ref, *out_refs):
    u = _rms_mod(h_ref[...], g_ref[...], mod_ref[1:2, :], mod_ref[0:1, :]).astype(BF16)
    off = 0
    for o_ref, wd in zip(out_refs, widths):
        o_ref[...] = jnp.dot(u, w_ref[:, off:off + wd], preferred_element_type=F32)
        off += wd


def _inproj(h, mod, g, w_bf16, widths, seq):
    t_rows = h.shape[0]
    tm = min(TM_PROJ, seq)
    per_b = seq // tm
    p = w_bf16.shape[1]
    assert sum(widths) == p
    return pl.pallas_call(
        functools.partial(_inproj_body, widths),
        out_shape=[jax.ShapeDtypeStruct((t_rows, wd), F32) for wd in widths],
        grid=(t_rows // tm,),
        in_specs=[
            pl.BlockSpec((tm, D), lambda i: (i, 0)),
            pl.BlockSpec((None, 6, D), lambda i: (i // per_b, 0, 0)),
            pl.BlockSpec((1, D), lambda i: (0, 0)),
            pl.BlockSpec((D, p), lambda i: (0, 0)),
        ],
        out_specs=[pl.BlockSpec((tm, wd), lambda i: (i, 0)) for wd in widths],
        compiler_params=_cparams(("arbitrary",)),
        name="inproj",
    )(h, mod, g.reshape(1, D), w_bf16)


def _causal_conv(x_ref, cw_ref, xbuf, first):
    tt = x_ref.shape[0]

    @pl.when(first)
    def _():
        xbuf[0:SUBLANES, :] = jnp.zeros((SUBLANES, xbuf.shape[1]), F32)

    xbuf[SUBLANES:SUBLANES + tt, :] = x_ref[...]
    xe = xbuf[...]
    acc = cw_ref[0:1, :] * xe
    for w in range(1, CONV_W):
        acc = pltpu.roll(acc, 1, 0) + cw_ref[w:w + 1, :] * xe
    xbuf[0:SUBLANES, :] = xbuf[tt:tt + SUBLANES, :]
    return acc[SUBLANES:, :]


def _rglru_body(xa_ref, ga_ref, cw_ref, cb_ref, wg_ref, bg_ref, lam_ref, o_ref, xbuf, hc):
    t = pl.program_id(1)
    tt = xa_ref.shape[0]

    @pl.when(t == 0)
    def _():
        hc[...] = jnp.zeros_like(hc)

    xc = _causal_conv(xa_ref, cw_ref, xbuf, t == 0) + cb_ref[...]
    gates = _dot(xc, wg_ref[...]) + bg_ref[...]
    r = _sigmoid(gates[:, :W_A])
    i = _sigmoid(gates[:, W_A:])
    log_a = -RG_C * r * _softplus(-lam_ref[...])
    a = jnp.exp(log_a)
    b = jnp.sqrt(1.0 - a * a) * (i * xc)
    sub = lax.broadcasted_iota(I32, (tt, W_A), 0) % SUBLANES
    s = 1
    while s < SUBLANES:
        a_sh = pltpu.roll(a, s, 0)
        b_sh = pltpu.roll(b, s, 0)
        valid = sub >= s
        b = jnp.where(valid, a * b_sh + b, b)
        a = jnp.where(valid, a * a_sh, a)
        s *= 2
    carry = hc[0:1, :]
    groups = []
    for grp in range(tt // SUBLANES):
        rows = slice(grp * SUBLANES, (grp + 1) * SUBLANES)
        hg = b[rows] + a[rows] * carry
        carry = hg[SUBLANES - 1:SUBLANES, :]
        groups.append(hg)
    h = jnp.concatenate(groups, axis=0)
    hc[...] = jnp.broadcast_to(carry, hc.shape)
    o_ref[...] = jax.nn.gelu(ga_ref[...], approximate=True) * h


def _rglru(xa, ga, conv_w, conv_b, wg_bd, bg, lam, bsz, seq):
    tt = min(TT_SEQ, seq)
    per_b = seq // tt
    row = lambda b, t: (b * per_b + t, 0)
    const = lambda b, t: (0, 0)
    return pl.pallas_call(
        _rglru_body,
        out_shape=jax.ShapeDtypeStruct(xa.shape, F32),
        grid=(bsz, per_b),
        in_specs=[
            pl.BlockSpec((tt, W_A), row),
            pl.BlockSpec((tt, W_A), row),
            pl.BlockSpec((CONV_W, W_A), const),
            pl.BlockSpec((1, W_A), const),
            pl.BlockSpec((W_A, 2 * W_A), const),
            pl.BlockSpec((1, 2 * W_A), const),
            pl.BlockSpec((1, W_A), const),
        ],
        out_specs=pl.BlockSpec((tt, W_A), row),
        scratch_shapes=[pltpu.VMEM((tt + SUBLANES, W_A), F32), pltpu.VMEM((SUBLANES, W_A), F32)],
        compiler_params=_cparams(("arbitrary", "arbitrary")),
        name="rglru",
    )(xa, ga, conv_w, conv_b.reshape(1, W_A), wg_bd, bg.reshape(1, 2 * W_A), lam.reshape(1, W_A))


def _gla_body(q_ref, k_ref, v_ref, og_ref, gl_ref, wg_ref, bg_ref, gn_ref, o_ref, st_ref):
    t = pl.program_id(0)
    nbat, tt = q_ref.shape[0], q_ref.shape[1]

    @pl.when(t == 0)
    def _():
        st_ref[...] = jnp.zeros_like(st_ref)

    ri = lax.broadcasted_iota(I32, (CHUNK, CHUNK), 0)
    ci = lax.broadcasted_iota(I32, (CHUNK, CHUNK), 1)
    tri_incl = (ri >= ci).astype(F32)

    nseq = nbat * NH_B

    def chunk_group(cg, carry):
        blocks = [(b * SUB, (b + 1) * SUB) for b in range(CHUNK // SUB)]
        rows_of = []
        qh, kh, gh, egh, vh, ogh = [], [], [], [], [], []
        for sub in range(GLA_GROUP):
            r0 = pl.multiple_of((cg * GLA_GROUP + sub) * CHUNK, CHUNK)
            rows = pl.ds(r0, CHUNK)
            rows_of.append(rows)
            for b in range(nbat):
                x = _dot_hi(gl_ref[b, rows, :], wg_ref[...]) + bg_ref[...]
                lg = -_softplus(-x) * (1.0 / GATE_NORM)
                g = _dot_hi(tri_incl, lg)
                eg = jnp.exp(g)
                q_all = q_ref[b, rows, :] * (DK_B ** -0.5)
                k_all = k_ref[b, rows, :]
                v_all = v_ref[b, rows, :]
                og_all = og_ref[b, rows, :]
                for hh in range(NH_B):
                    ks_ = slice(hh * DKP_B, (hh + 1) * DKP_B)
                    vs_ = slice(hh * DV_B, (hh + 1) * DV_B)
                    qh.append(q_all[:, ks_])
                    kh.append(k_all[:, ks_])
                    gh.append(g[:, ks_])
                    egh.append(eg[:, ks_])
                    vh.append(v_all[:, vs_])
                    ogh.append(og_all[:, vs_])
        every = range(GLA_GROUP * nseq)
        g_last = [gh[h][CHUNK - 1:CHUNK, :] for h in every]
        att = [[None] * len(blocks) for _ in every]
        for h in every:
            for b, (lo, hi) in enumerate(blocks):
                ref_g = gh[h][lo:lo + 1, :]
                qb = qh[h][lo:hi] * jnp.exp(gh[h][lo:hi] - ref_g)
                kb = kh[h][:hi] * jnp.exp(ref_g - gh[h][:hi])
                causal = (lax.broadcasted_iota(I32, (SUB, hi), 1)
                          <= lax.broadcasted_iota(I32, (SUB, hi), 0) + lo)
                att[h][b] = jnp.where(causal, _dot_nt(qb, kb), 0.0)
        upd = [_dot_tn(vh[h], kh[h] * jnp.exp(g_last[h] - gh[h])) for h in every]
        o_intra = [[_dot(att[h][b], vh[h][:hi]) for b, (lo, hi) in enumerate(blocks)] for h in every]
        st = [st_ref[s] for s in range(nseq)]
        for sub in range(GLA_GROUP):
            heads = range(sub * nseq, (sub + 1) * nseq)
            o_inter = {h: _dot_nt(qh[h] * egh[h], st[h - sub * nseq]) for h in heads}
            outs = []
            for h in heads:
                st[h - sub * nseq] = st[h - sub * nseq] * jnp.exp(g_last[h]) + upd[h]
                o = o_inter[h] + jnp.concatenate(o_intra[h], axis=0)
                o = o * lax.rsqrt(jnp.mean(o * o, axis=-1, keepdims=True) + EPS) * gn_ref[...]
                outs.append(o * _silu(ogh[h]))
            for b in range(nbat):
                o_ref[b, rows_of[sub], :] = jnp.concatenate(outs[b * NH_B:(b + 1) * NH_B], axis=1)
        for s in range(nseq):
            st_ref[s] = st[s]
        return carry

    lax.fori_loop(0, tt // (CHUNK * GLA_GROUP), chunk_group, 0)


def _gla(q, k, v, og, gl, wg2p, bg2p, gnorm, bsz, seq):
    tt = min(TT_SEQ, seq)
    row = lambda t: (0, t, 0)
    const = lambda t: (0, 0)
    r3 = lambda a: a.reshape(bsz, seq, a.shape[-1])
    out = pl.pallas_call(
        _gla_body,
        out_shape=jax.ShapeDtypeStruct((bsz, seq, V_B), F32),
        grid=(seq // tt,),
        in_specs=[
            pl.BlockSpec((bsz, tt, KP_B), row),
            pl.BlockSpec((bsz, tt, KP_B), row),
            pl.BlockSpec((bsz, tt, V_B), row),
            pl.BlockSpec((bsz, tt, V_B), row),
            pl.BlockSpec((bsz, tt, LANES), row),
            pl.BlockSpec((LANES, KP_B), const),
            pl.BlockSpec((1, KP_B), const),
            pl.BlockSpec((1, DV_B), const),
        ],
        out_specs=pl.BlockSpec((bsz, tt, V_B), row),
        scratch_shapes=[pltpu.VMEM((bsz * NH_B, DV_B, DKP_B), F32)],
        compiler_params=_cparams(("arbitrary",)),
        name="gla",
    )(r3(q), r3(k), r3(v), r3(og), r3(gl), wg2p, bg2p, gnorm.reshape(1, DV_B))
    return out.reshape(bsz * seq, V_B)


def _unit_lower_solve(lows, rhss, ri, ci):
    n = range(len(lows))
    same = (ri // SUB) == (ci // SUB)
    eye = (ri == ci).astype(F32)
    dg = [jnp.where(same, lows[h], 0.0) for h in n]
    off = [lows[h] - dg[h] for h in n]
    tinv = [eye - dg[h] for h in n]
    pw = dg
    e = 2
    while e < SUB:
        pw = [_dot(pw[h], pw[h]) for h in n]
        tinv = [tinv[h] + _dot(tinv[h], pw[h]) for h in n]
        e *= 2
    nmat = [_dot(tinv[h], off[h]) for h in n]
    x = [_dot(tinv[h], rhss[h]) for h in n]
    sol = [[x[h][0:SUB]] for h in n]
    for blk in range(1, CHUNK // SUB):
        lo, hi = blk * SUB, (blk + 1) * SUB
        for h in n:
            prev = jnp.concatenate(sol[h], axis=0)
            sol[h].append(x[h][lo:hi] - _dot(nmat[h][lo:hi, :lo], prev))
    return [jnp.concatenate(sol[h], axis=0) for h in n]


def _delta_body(q_ref, k_ref, v_ref, z_ref, ba_ref, cq_ref, ck_ref, cv_ref, alog_ref, dtb_ref, dn_ref,
                o_ref, qbuf, kbuf, vbuf, qs, ks, vs, st_ref):
    t = pl.program_id(0)
    nbat, tt = q_ref.shape[0], q_ref.shape[1]

    @pl.when(t == 0)
    def _():
        st_ref[...] = jnp.zeros_like(st_ref)

    for b in range(nbat):
        qs[b] = _silu(_causal_conv(q_ref.at[b], cq_ref, qbuf.at[b], t == 0))
        ks[b] = _silu(_causal_conv(k_ref.at[b], ck_ref, kbuf.at[b], t == 0))
        vs[b] = _silu(_causal_conv(v_ref.at[b], cv_ref, vbuf.at[b], t == 0))

    ri = lax.broadcasted_iota(I32, (CHUNK, CHUNK), 0)
    ci = lax.broadcasted_iota(I32, (CHUNK, CHUNK), 1)
    tri_incl = (ri >= ci).astype(F32)
    incl = ri >= ci
    strict = ri > ci

    nseq = nbat * NH_C

    def chunk_group(cg, carry):
        rows_of, seqs = [], []
        qh, kh, vh, zh, beta, gcol, egcol, g_last, decay = [], [], [], [], [], [], [], [], []
        for sub in range(DELTA_GROUP):
            r0 = pl.multiple_of((cg * DELTA_GROUP + sub) * CHUNK, CHUNK)
            rows = pl.ds(r0, CHUNK)
            rows_of.append(rows)
            for b in range(nbat):
                ba = ba_ref[b, rows, :]
                beta_all = _sigmoid(ba)
                g_all = -jnp.exp(alog_ref[...]) * _softplus(ba + dtb_ref[...])
                gcum = _dot_hi(tri_incl, g_all)
                gcum_t = gcum.T
                egc = jnp.exp(gcum)
                q_all, k_all, v_all, z_all = qs[b, rows, :], ks[b, rows, :], vs[b, rows, :], z_ref[b, rows, :]
                for hh in range(NH_C):
                    hs = slice(hh * DH_C, (hh + 1) * DH_C)
                    q_, k_ = q_all[:, hs], k_all[:, hs]
                    qh.append(q_ * lax.rsqrt(jnp.sum(q_ * q_, axis=-1, keepdims=True) + EPS) * (DH_C ** -0.5))
                    kh.append(k_ * lax.rsqrt(jnp.sum(k_ * k_, axis=-1, keepdims=True) + EPS))
                    vh.append(v_all[:, hs])
                    zh.append(z_all[:, hs])
                    beta.append(beta_all[:, hh:hh + 1])
                    gcol.append(gcum[:, NH_C + hh:NH_C + hh + 1])
                    egcol.append(egc[:, NH_C + hh:NH_C + hh + 1])
                    g_last.append(gcum[CHUNK - 1:CHUNK, NH_C + hh:NH_C + hh + 1])
                    grow = gcum_t[NH_C + hh:NH_C + hh + 1, :]
                    decay.append(jnp.exp(jnp.where(incl, gcol[-1] - grow, -jnp.inf)))
        every = range(DELTA_GROUP * nseq)
        kbeta = [kh[h] * beta[h] for h in every]
        kq = [_dot_nt(jnp.concatenate([kbeta[h], qh[h]], axis=0), kh[h]) for h in every]
        low = [jnp.where(strict, kq[h][:CHUNK] * decay[h], 0.0) for h in every]
        rhs = [jnp.concatenate([vh[h] * beta[h], kbeta[h] * egcol[h]], axis=1) for h in every]
        sol = _unit_lower_solve(low, rhs, ri, ci)
        st = [st_ref[s] for s in range(nseq)]
        for sub in range(DELTA_GROUP):
            heads = range(sub * nseq, (sub + 1) * nseq)
            ws = {h: _dot(jnp.concatenate([sol[h][:, DH_C:], qh[h] * egcol[h]], axis=0), st[h - sub * nseq])
                  for h in heads}
            v_new = {h: sol[h][:, :DH_C] - ws[h][:CHUNK] for h in heads}
            o_intra = {h: _dot(kq[h][CHUNK:] * decay[h], v_new[h]) for h in heads}
            upd = {h: _dot_tn(kh[h] * jnp.exp(g_last[h] - gcol[h]), v_new[h]) for h in heads}
            outs = []
            for h in heads:
                st[h - sub * nseq] = st[h - sub * nseq] * jnp.exp(g_last[h]) + upd[h]
                o = ws[h][CHUNK:] + o_intra[h]
                o = o * lax.rsqrt(jnp.mean(o * o, axis=-1, keepdims=True) + EPS) * dn_ref[...]
                outs.append(o * _silu(zh[h]))
            for b in range(nbat):
                o_ref[b, rows_of[sub], :] = jnp.concatenate(outs[b * NH_C:(b + 1) * NH_C], axis=1)
        for s in range(nseq):
            st_ref[s] = st[s]
        return carry

    lax.fori_loop(0, tt // (CHUNK * DELTA_GROUP), chunk_group, 0)


def _delta(q, k, v, z, ba, cq, ck, cv, alog, dtb, dnorm, bsz, seq):
    tt = min(TT_SEQ, seq)
    row = lambda t: (0, t, 0)
    const = lambda t: (0, 0)
    big = pl.BlockSpec((bsz, tt, W_C), row)
    cw = pl.BlockSpec((CONV_W, W_C), const)
    vec = pl.BlockSpec((1, LANES), const)
    r3 = lambda a: a.reshape(bsz, seq, a.shape[-1])
    out = pl.pallas_call(
        _delta_body,
        out_shape=jax.ShapeDtypeStruct((bsz, seq, W_C), F32),
        grid=(seq // tt,),
        in_specs=[big, big, big, big, pl.BlockSpec((bsz, tt, LANES), row), cw, cw, cw, vec, vec, vec],
        out_specs=big,
        scratch_shapes=[pltpu.VMEM((bsz, tt + SUBLANES, W_C), F32)] * 3 + [pltpu.VMEM((bsz, tt, W_C), F32)] * 3
        + [pltpu.VMEM((bsz * NH_C, DH_C, DH_C), F32)],
        compiler_params=_cparams(("arbitrary",)),
        name="delta",
    )(r3(q), r3(k), r3(v), r3(z), r3(ba), cq, ck, cv, alog, dtb, dnorm)
    return out.reshape(bsz * seq, W_C)


def _outproj_router_body(n_parts, *refs):
    y_refs = refs[:n_parts]
    w_refs = refs[n_parts:2 * n_parts]
    (h_ref, mod_ref, g2_ref, wr_hi_ref, wr_lo_ref, br_ref,
     h1_ref, u2_ref, info_ref, cnt_ref, carry, uhi_s, ulo_s) = refs[2 * n_parts:]
    i = pl.program_id(0)
    tm = h_ref.shape[0]

    @pl.when(i == 0)
    def _():
        carry[...] = jnp.zeros_like(carry)
        uhi_s[...] = jnp.zeros_like(uhi_s)
        ulo_s[...] = jnp.zeros_like(ulo_s)

    u_hi = uhi_s[...]
    u_lo = ulo_s[...]
    w_hi = wr_hi_ref[...]
    logits = (jnp.dot(u_hi, w_hi, preferred_element_type=F32)
              + jnp.dot(u_hi, wr_lo_ref[...], preferred_element_type=F32)
              + jnp.dot(u_lo, w_hi, preferred_element_type=F32)) + br_ref[...]
    m = None
    for y_ref, w_ref in zip(y_refs, w_refs):
        part = _dot(y_ref[...], w_ref[...])
        m = part if m is None else m + part
    h1 = h_ref[...] + mod_ref[2:3, :] * m
    h1_ref[...] = h1
    u2 = _rms_mod(h1, g2_ref[...], mod_ref[4:5, :], mod_ref[3:4, :])
    _to_row_tiles(u2_ref, u2)
    nxt_hi = u2.astype(BF16)
    uhi_s[...] = nxt_hi
    ulo_s[...] = (u2 - nxt_hi.astype(F32)).astype(BF16)

    live = (i > 0).astype(F32)
    lane = lax.broadcasted_iota(I32, (tm, LANES), 1)
    neg = -jnp.inf
    count = carry[0:1, :]
    lgp = jnp.where((lane >= N_EXP) & (lane < N_EXP + N_GROUPS), logits, neg)
    gmax = jnp.max(lgp, axis=-1, keepdims=True)
    pg_top = 1.0 / jnp.sum(jnp.exp(lgp - gmax), axis=-1, keepdims=True)
    g_idx = jnp.min(jnp.where(lgp == gmax, lane, LANES), axis=-1, keepdims=True) - N_EXP
    in_group = (lane // EPG == g_idx) & (lane < N_EXP)
    le = jnp.where(in_group, logits, neg)
    emax = jnp.max(le, axis=-1, keepdims=True)
    pe = jnp.exp(le - emax)
    pe = jnp.where(in_group, pe / jnp.sum(pe, axis=-1, keepdims=True), -1.0)
    p0 = jnp.max(pe, axis=-1, keepdims=True)
    e0 = jnp.min(jnp.where(pe == p0, lane, LANES), axis=-1, keepdims=True)
    pe1 = jnp.where(lane == e0, -1.0, pe)
    p1 = jnp.max(pe1, axis=-1, keepdims=True)
    e1 = jnp.min(jnp.where(pe1 == p1, lane, LANES), axis=-1, keepdims=True)
    den = p0 + p1
    w0 = pg_top * p0 / den
    w1 = pg_top * p1 / den
    hot = ((lane == e0) | (lane == e1)).astype(F32) * live
    ri = lax.broadcasted_iota(I32, (tm, tm), 0)
    ci = lax.broadcasted_iota(I32, (tm, tm), 1)
    before = _dot((ri > ci).astype(F32), hot) + count
    r0 = jnp.sum(jnp.where(lane == e0, before, 0.0), axis=-1, keepdims=True)
    r1 = jnp.sum(jnp.where(lane == e1, before, 0.0), axis=-1, keepdims=True)
    info = jnp.where(lane == 0, e0.astype(F32), 0.0)
    info = jnp.where(lane == 1, e1.astype(F32), info)
    info = jnp.where(lane == 2, r0, info)
    info = jnp.where(lane == 3, r1, info)
    info = jnp.where(lane == 4, w0, info)
    info = jnp.where(lane == 5, w1, info)
    info_ref[...] = info
    carry[...] = jnp.broadcast_to(count + jnp.sum(hot, axis=0, keepdims=True), carry.shape)
    cnt_ref[...] = carry[...]


def _outproj_router(ys, ws, h, mod, g2, wr_hi, wr_lo, br, seq):
    t_rows = h.shape[0]
    tm = min(TM_PROJ, seq)
    per_b = seq // tm
    n = len(ys)
    nt = t_rows // tm
    row = lambda i: (jnp.minimum(i, nt - 1), 0)
    prev = lambda i: (jnp.maximum(i - 1, 0), 0)
    const = lambda i: (0, 0)
    in_specs = [pl.BlockSpec((tm, y.shape[1]), row) for y in ys]
    in_specs += [pl.BlockSpec(w.shape, const) for w in ws]
    in_specs += [
        pl.BlockSpec((tm, D), row),
        pl.BlockSpec((None, 6, D), lambda i: (jnp.minimum(i, nt - 1) // per_b, 0, 0)),
        pl.BlockSpec((1, D), const),
        pl.BlockSpec((D, LANES), const),
        pl.BlockSpec((D, LANES), const),
        pl.BlockSpec((1, LANES), const),
    ]
    return pl.pallas_call(
        functools.partial(_outproj_router_body, n),
        out_shape=[
            jax.ShapeDtypeStruct((t_rows, D), F32),
            jax.ShapeDtypeStruct((t_rows * RT, LANES), F32),
            jax.ShapeDtypeStruct((t_rows, LANES), F32),
            jax.ShapeDtypeStruct((SUBLANES, LANES), F32),
        ],
        grid=(nt + 1,),
        in_specs=in_specs,
        out_specs=[
            pl.BlockSpec((tm, D), row),
            pl.BlockSpec((tm * RT, LANES), row),
            pl.BlockSpec((tm, LANES), prev),
            pl.BlockSpec((SUBLANES, LANES), const),
        ],
        scratch_shapes=[pltpu.VMEM((SUBLANES, LANES), F32), pltpu.VMEM((tm, D), BF16), pltpu.VMEM((tm, D), BF16)],
        compiler_params=_cparams(("arbitrary",)),
        name="outproj_router",
    )(*ys, *ws, h, mod, g2.reshape(1, D), wr_hi, wr_lo, br)


RT = D // LANES
assert RT == SUBLANES


def _to_row_tiles(ref, x, row0=0):
    n = x.shape[0]
    for c in range(RT):
        ref[pl.ds(row0 * RT + c, n, stride=RT), :] = x[:, c * LANES:(c + 1) * LANES]


def _from_row_tiles(ref, n):
    return jnp.concatenate([ref[pl.ds(c, n, stride=RT), :] for c in range(RT)], axis=1)


def _row_copy(src, src_row, dst, dst_row, sem):
    return pltpu.make_async_copy(src.at[pl.ds(pl.multiple_of(src_row * RT, RT), RT)],
                                 dst.at[pl.ds(pl.multiple_of(dst_row * RT, RT), RT)], sem)


def _pos_body(info_ref, sp_ref, o_ref):
    info = info_ref[...]
    td = info.shape[0]
    lane = lax.broadcasted_iota(I32, (td, LANES), 1)
    sp = sp_ref[...]
    packed = jnp.zeros((td, LANES), F32)
    for k in range(TOP_K):
        e = info[:, k:k + 1].astype(I32)
        start = jnp.sum(jnp.where(lane == e, sp, 0.0), axis=-1, keepdims=True)
        packed = jnp.where(lane == k, start + info[:, TOP_K + k:TOP_K + k + 1], packed)
    pick = (lax.broadcasted_iota(I32, (SUBLANES, LANES), 0)
            == lax.broadcasted_iota(I32, (SUBLANES, LANES), 1)).astype(F32)
    o_ref[...] = lax.dot_general(pick, packed, (((1,), (1,)), ((), ())), precision=HI,
                                 preferred_element_type=F32).astype(I32)


def _moe_pos(info, sp_row):
    t_rows = info.shape[0]
    tp = min(TP_MOE, t_rows)
    return pl.pallas_call(
        _pos_body,
        out_shape=jax.ShapeDtypeStruct((SUBLANES, t_rows), I32),
        grid=(t_rows // tp,),
        in_specs=[pl.BlockSpec((tp, LANES), lambda i: (i, 0)), pl.BlockSpec((1, LANES), lambda i: (0, 0))],
        out_specs=pl.BlockSpec((SUBLANES, tp), lambda i: (0, i)),
        compiler_params=_cparams(("arbitrary",)),
        name="moe_pos",
    )(info, sp_row)


def _dispatch_body(be_ref, nu_ref, pos_ref, u_ref, xb_out, zero_s, sems):
    i = pl.program_id(0)
    td = u_ref.shape[0] // RT
    nb = be_ref.shape[0]

    @pl.when(i == 0)
    def _():
        zero_s[...] = jnp.zeros_like(zero_s)

        def needs_zero(b):
            return (b >= nu_ref[0] - 1) | (be_ref[jnp.minimum(b + 1, nb - 1)] != be_ref[b])

        def block_copy(b):
            return pltpu.make_async_copy(zero_s, xb_out.at[pl.ds(b * (BM_MOE * RT), BM_MOE * RT)], sems.at[1])

        def zissue(b, c):
            @pl.when(needs_zero(b))
            def _():
                block_copy(b).start()
            return c

        def zdrain(b, c):
            @pl.when(needs_zero(b))
            def _():
                block_copy(b).wait()
            return c

        lax.fori_loop(0, nb, zissue, 0)
        lax.fori_loop(0, nb, zdrain, 0)

    def issue(j, c):
        for k in range(TOP_K):
            _row_copy(u_ref, j, xb_out, pos_ref[k, j], sems.at[0]).start(priority=k)
        return c

    lax.fori_loop(0, td, issue, 0, unroll=DMA_UNROLL)

    def drain(j, c):
        for k in range(TOP_K):
            _row_copy(u_ref, j, xb_out, pos_ref[k, j], sems.at[0]).wait()
        return c

    lax.fori_loop(0, td, drain, 0, unroll=DMA_UNROLL)


def _dispatch(blk_e, n_used, pos, u2, n_rows, td):
    t_rows = u2.shape[0] // RT
    nstep = t_rows // td
    return pl.pallas_call(
        _dispatch_body,
        out_shape=jax.ShapeDtypeStruct((n_rows * RT, LANES), F32),
        grid_spec=pltpu.PrefetchScalarGridSpec(
            num_scalar_prefetch=2,
            grid=(nstep,),
            in_specs=[
                pl.BlockSpec((SUBLANES, td), lambda i, be, nu: (0, i), memory_space=pltpu.SMEM),
                pl.BlockSpec((td * RT, LANES), lambda i, be, nu: (i, 0)),
            ],
            out_specs=pl.BlockSpec(memory_space=pl.ANY),
            scratch_shapes=[pltpu.VMEM((BM_MOE * RT, LANES), F32), pltpu.SemaphoreType.DMA((2,))],
        ),
        compiler_params=_cparams(("arbitrary",)),
        name="moe_dispatch",
    )(blk_e, n_used, pos, u2)


def _ffn_body(be_ref, nu_ref, x_ref, w1_ref, w3_ref, w2_ref, o_ref, w13_s, w2_s):
    i = pl.program_id(0)
    prev = be_ref[jnp.maximum(i - 1, 0)]
    fresh = (i == 0) | (be_ref[i] != prev)

    @pl.when(fresh & (i < nu_ref[0]))
    def _():
        w13_s[:, :D_EXP] = w1_ref[...].astype(BF16)
        w13_s[:, D_EXP:] = w3_ref[...].astype(BF16)
        w2_s[...] = w2_ref[...].astype(BF16)

    @pl.when(i < nu_ref[0])
    def _():
        x = _from_row_tiles(x_ref, BM_MOE).astype(BF16)
        a = jnp.dot(x, w13_s[...], preferred_element_type=F32)
        hid = _silu(a[:, :D_EXP]) * a[:, D_EXP:]
        _to_row_tiles(o_ref, jnp.dot(hid.astype(BF16), w2_s[...], preferred_element_type=F32))

    @pl.when(i >= nu_ref[0])
    def _():
        o_ref[...] = jnp.zeros_like(o_ref)


def _ffn(blk_e, n_used, xb, w1, w3, w2, layer):
    n_rows = xb.shape[0] // RT
    nb = n_rows // BM_MOE

    def last_used(i, nu):
        return jnp.maximum(jnp.minimum(i, nu[0] - 1), 0)

    def xmap(i, be, nu):
        return (last_used(i, nu), 0)

    def wmap(i, be, nu):
        return (layer, be[last_used(i, nu)], 0, 0)

    return pl.pallas_call(
        _ffn_body,
        out_shape=jax.ShapeDtypeStruct((n_rows * RT, LANES), F32),
        grid_spec=pltpu.PrefetchScalarGridSpec(
            num_scalar_prefetch=2,
            grid=(nb,),
            in_specs=[
                pl.BlockSpec((BM_MOE * RT, LANES), xmap),
                pl.BlockSpec((None, None, D, D_EXP), wmap),
                pl.BlockSpec((None, None, D, D_EXP), wmap),
                pl.BlockSpec((None, None, D_EXP, D), wmap),
            ],
            out_specs=pl.BlockSpec((BM_MOE * RT, LANES), lambda i, be, nu: (i, 0)),
            scratch_shapes=[pltpu.VMEM((D, 2 * D_EXP), BF16), pltpu.VMEM((D_EXP, D), BF16)],
        ),
        compiler_params=_cparams(("arbitrary",)),
        name="moe_ffn",
    )(blk_e, n_used, xb, w1, w3, w2)


def _combine_body(final, pos_ref, nxt_ref, yb_hbm, h_ref, info_ref, mod_ref, fn_ref, o_ref, ybuf, sems):
    i = pl.program_id(0)
    n = pl.num_programs(0)
    td = h_ref.shape[0]
    slot = i % 2

    def gather(p_ref, s, wait):
        def body(j, c):
            for k in range(TOP_K):
                cp = _row_copy(yb_hbm, p_ref[k, j], ybuf.at[s, k], j, sems.at[s])
                cp.wait() if wait else cp.start(priority=k)
            return c

        lax.fori_loop(0, td, body, 0, unroll=DMA_UNROLL)

    @pl.when(i == 0)
    def _():
        gather(pos_ref, slot, False)

    @pl.when(i + 1 < n)
    def _():
        gather(nxt_ref, 1 - slot, False)

    gather(pos_ref, slot, True)
    info = info_ref[...]
    y = (info[:, 4:5] * _from_row_tiles(ybuf.at[slot, 0], td)
         + info[:, 5:6] * _from_row_tiles(ybuf.at[slot, 1], td))
    h2 = h_ref[...] + mod_ref[5:6, :] * y
    if final:
        h2 = h2 * lax.rsqrt(jnp.mean(h2 * h2, axis=-1, keepdims=True) + EPS) * fn_ref[...]
    o_ref[...] = h2


def _combine(pos, yb, h1, info, mod, fnorm, seq, td, final):
    t_rows = h1.shape[0]
    per_b = seq // td
    nstep = t_rows // td
    row = lambda i: (i, 0)
    return pl.pallas_call(
        functools.partial(_combine_body, final),
        out_shape=jax.ShapeDtypeStruct((t_rows, D), F32),
        grid=(nstep,),
        in_specs=[
            pl.BlockSpec((SUBLANES, td), lambda i: (0, i), memory_space=pltpu.SMEM),
            pl.BlockSpec((SUBLANES, td), lambda i: (0, jnp.minimum(i + 1, nstep - 1)),
                         memory_space=pltpu.SMEM),
            pl.BlockSpec(memory_space=pl.ANY),
            pl.BlockSpec((td, D), row),
            pl.BlockSpec((td, LANES), row),
            pl.BlockSpec((None, 6, D), lambda i: (i // per_b, 0, 0)),
            pl.BlockSpec((1, D), lambda i: (0, 0)),
        ],
        out_specs=pl.BlockSpec((td, D), row),
        scratch_shapes=[pltpu.VMEM((2, TOP_K, td * RT, LANES), F32), pltpu.SemaphoreType.DMA((2,))],
        compiler_params=_cparams(("arbitrary",)),
        name="moe_combine",
    )(pos, pos, yb, h1, info, mod, fnorm.reshape(1, D))


def _moe(u2, info, counts, h1, mod, w1, w3, w2, layer, fnorm, seq, final):
    t_rows = h1.shape[0]
    td = min(TD_MOE, seq)
    nb = (t_rows * TOP_K) // BM_MOE + N_EXP
    n_rows = nb * BM_MOE
    cnt = counts[0, :N_EXP].astype(I32)
    padded = (cnt + BM_MOE - 1) // BM_MOE * BM_MOE
    end_p = jnp.cumsum(padded)
    start_p = end_p - padded
    sp_row = jnp.zeros((1, LANES), F32).at[0, :N_EXP].set(start_p.astype(F32))
    blk_e = jnp.minimum(jnp.sum(jnp.arange(nb, dtype=I32)[:, None] * BM_MOE >= end_p[None, :], axis=1),
                        N_EXP - 1).astype(I32)
    n_used = (end_p[-1:] // BM_MOE).astype(I32)
    pos = _moe_pos(info, sp_row)
    xb = _dispatch(blk_e, n_used, pos, u2, n_rows, td)
    yb = _ffn(blk_e, n_used, xb, w1, w3, w2, layer)
    return _combine(pos, yb, h1, info, mod, fnorm, seq, td, final)


def _pad_cols(w, n):
    return jnp.pad(w, ((0, 0), (0, n - w.shape[1])))


def _pad_heads(w):
    w = w.reshape(w.shape[:-1] + (NH_B, DK_B))
    w = jnp.pad(w, [(0, 0)] * (w.ndim - 1) + [(0, DKP_B - DK_B)])
    return w.reshape(w.shape[:-2] + (KP_B,))


def _block_diag(w):
    nh, hw, _ = w.shape
    eye = jnp.eye(nh, dtype=w.dtype)
    return (eye[:, None, :, None] * w[:, :, None, :]).reshape(nh * hw, nh * hw)


def _router_weights(w_grp, b_grp, w_rt, b_rt):
    w = _pad_cols(jnp.concatenate([w_rt, w_grp], axis=1), LANES)
    b = _pad_cols(jnp.concatenate([b_rt, b_grp]).reshape(1, -1), LANES)
    w_hi = w.astype(BF16)
    w_lo = (w - w_hi.astype(F32)).astype(BF16)
    return w_hi, w_lo, b


def kernel(x, c, norm1, norm2, w_ada, b_ada, w_in_ab, conv_a_w, conv_a_b, rg_wa, rg_ba, rg_wx, rg_bx, rg_lam, gla_wg2, gla_bg2, gla_norm, w_out_ab, w_in_c, conv_c_w, dn_a_log, dn_dt_bias, dn_norm, w_out_c, moe_w_grp, moe_b_grp, moe_w_rt, moe_b_rt, moe_w1, moe_w3, moe_w2, final_norm):
    bsz, seq, _ = x.shape
    depth = w_ada.shape[0]
    t_rows = bsz * seq
    mod_all = _ada_mod(c, w_ada, b_ada)
    h = x.reshape(t_rows, D)
    for layer in range(depth):
        mod = mod_all[layer]
        j = layer // 2
        if layer % 2 == 0:
            wi = w_in_ab[j]
            o_q, o_k, o_v, o_gl = 2 * W_A, 2 * W_A + K_B, 2 * W_A + 2 * K_B, 2 * W_A + 2 * K_B + 2 * V_B
            w_in = jnp.concatenate([
                wi[:, :o_q], _pad_heads(wi[:, o_q:o_k]), _pad_heads(wi[:, o_k:o_v]), wi[:, o_v:o_gl],
                _pad_cols(wi[:, o_gl:], LANES)], axis=1).astype(BF16)
            xa, ga, q, k, v, og, gl = _inproj(h, mod, norm1[layer], w_in,
                                              (W_A, W_A, KP_B, KP_B, V_B, V_B, LANES), seq)
            wg_bd = jnp.concatenate([_block_diag(rg_wa[j]), _block_diag(rg_wx[j])], axis=1).astype(BF16)
            bg = jnp.concatenate([rg_ba[j], rg_bx[j]])
            ya = _rglru(xa, ga, conv_a_w[j], conv_a_b[j], wg_bd, bg, rg_lam[j], bsz, seq)
            wg2p = jnp.pad(_pad_heads(gla_wg2[j]), ((0, LANES - R_GATE), (0, 0)))
            bg2p = _pad_heads(gla_bg2[j].reshape(1, K_B))
            ob = _gla(q, k, v, og, gl, wg2p, bg2p, gla_norm[j], bsz, seq)
            ys = (ya, ob)
            ws = (w_out_ab[j][:W_A].astype(BF16), w_out_ab[j][W_A:].astype(BF16))
        else:
            p_c = w_in_c.shape[2]
            w_in = _pad_cols(w_in_c[j], p_c - 2 * NH_C + LANES).astype(BF16)
            q, k, v, z, ba = _inproj(h, mod, norm1[layer], w_in, (W_C, W_C, W_C, W_C, LANES), seq)
            cw = conv_c_w[j]
            lanes_c = jnp.zeros((1, LANES), F32)
            alog = lanes_c.at[0, NH_C:2 * NH_C].set(dn_a_log[j])
            dtb = lanes_c.at[0, NH_C:2 * NH_C].set(dn_dt_bias[j])
            o = _delta(q, k, v, z, ba, cw[:, :W_C], cw[:, W_C:2 * W_C], cw[:, 2 * W_C:], alog, dtb,
                       dn_norm[j].reshape(1, DH_C), bsz, seq)
            ys = (o,)
            ws = (w_out_c[j].astype(BF16),)
        rw = _router_weights(moe_w_grp[layer], moe_b_grp[layer], moe_w_rt[layer], moe_b_rt[layer])
        h1, u2, info, counts = _outproj_router(ys, ws, h, mod, norm2[layer], *rw, seq)
        h = _moe(u2, info, counts, h1, mod, moe_w1, moe_w3, moe_w2, layer, final_norm, seq,
                 final=layer == depth - 1)
    return h.reshape(bsz, seq, D)
```

```python
import functools

import jax
import jax.numpy as jnp
from jax import lax
from jax.experimental import pallas as pl
from jax.experimental.pallas import tpu as pltpu

F32 = jnp.float32
BF16 = jnp.bfloat16
I32 = jnp.int32
HI = lax.Precision.HIGHEST

D = 1024
CHUNK = 64
SUB = 16
CONV_W = 4
EPS = 1e-6
LANES = 128
SUBLANES = 8

W_A = D // 2
NH_A = 8
HW_A = W_A // NH_A
RG_C = 8.0
NH_B = 4
DK_B = 64
DV_B = 128
K_B = NH_B * DK_B
V_B = NH_B * DV_B
DKP_B = LANES
KP_B = NH_B * DKP_B
R_GATE = 16
GATE_NORM = 16.0

NH_C = 8
DH_C = 128
W_C = NH_C * DH_C

N_GROUPS = 4
EPG = 8
N_EXP = N_GROUPS * EPG
TOP_K = 2
D_EXP = 512

VMEM_LIMIT = 56 * 1024 * 1024

TM_PROJ = 512
TT_SEQ = 256
BM_MOE = 512
TD_MOE = 512
TP_MOE = 2048
DMA_UNROLL = 8
DELTA_GROUP = 4
GLA_GROUP = 4


def _cparams(sem):
    return pltpu.CompilerParams(dimension_semantics=sem, vmem_limit_bytes=VMEM_LIMIT)


def _softplus(x):
    return jnp.maximum(x, 0.0) + jnp.log1p(jnp.exp(-jnp.abs(x)))


def _sigmoid(x):
    return jax.nn.sigmoid(x)


def _silu(x):
    return x * jax.nn.sigmoid(x)


def _rms_mod(h, g, sc, sh):
    y = h * lax.rsqrt(jnp.mean(h * h, axis=-1, keepdims=True) + EPS)
    return y * g * (1.0 + sc) + sh


def _dot(a, b):
    return jnp.dot(a.astype(BF16), b.astype(BF16), preferred_element_type=F32)


def _dot_nt(a, b):
    return lax.dot_general(a.astype(BF16), b.astype(BF16), (((1,), (1,)), ((), ())),
                           preferred_element_type=F32)


def _dot_tn(a, b):
    return lax.dot_general(a.astype(BF16), b.astype(BF16), (((0,), (0,)), ((), ())),
                           preferred_element_type=F32)


def _dot_hi(a, b):
    return jnp.dot(a, b, precision=HI, preferred_element_type=F32)


def _ada_body(c_ref, w_ref, b_ref, o_ref):
    o_ref[...] = _dot_hi(_silu(c_ref[...]), w_ref[...]) + b_ref[...]


def _ada_mod(c, w_ada, b_ada):
    depth, _, n6 = w_ada.shape
    bsz = c.shape[0]
    nt = 1536
    c8 = jnp.zeros((SUBLANES, D), F32).at[:bsz].set(c)
    out = pl.pallas_call(
        _ada_body,
        out_shape=jax.ShapeDtypeStruct((depth, SUBLANES, n6), F32),
        grid=(depth, n6 // nt),
        in_specs=[
            pl.BlockSpec((SUBLANES, D), lambda l, j: (0, 0)),
            pl.BlockSpec((None, D, nt), lambda l, j: (l, 0, j)),
            pl.BlockSpec((None, 1, nt), lambda l, j: (l, 0, j)),
        ],
        out_specs=pl.BlockSpec((None, SUBLANES, nt), lambda l, j: (l, 0, j)),
        compiler_params=_cparams(("arbitrary", "arbitrary")),
        name="ada_mod",
    )(c8, w_ada, b_ada.reshape(depth, 1, n6))
    return out[:, :bsz].reshape(depth, bsz, 6, D)


def _inproj_body(widths, h_ref, mod_ref, g_ref, w_ref, *out_refs):
    u = _rms_mod(h_ref[...], g_ref[...], mod_ref[1:2, :], mod_ref[0:1, :]).astype(BF16)
    off = 0
    for o_ref, wd in zip(out_refs, widths):
        o_ref[...] = jnp.dot(u, w_ref[:, off:off + wd], preferred_element_type=F32)
        off += wd


def _inproj(h, mod, g, w_bf16, widths, seq):
    t_rows = h.shape[0]
    tm = min(TM_PROJ, seq)
    per_b = seq // tm
    p = w_bf16.shape[1]
    assert sum(widths) == p
    return pl.pallas_call(
        functools.partial(_inproj_body, widths),
        out_shape=[jax.ShapeDtypeStruct((t_rows, wd), F32) for wd in widths],
        grid=(t_rows // tm,),
        in_specs=[
            pl.BlockSpec((tm, D), lambda i: (i, 0)),
            pl.BlockSpec((None, 6, D), lambda i: (i // per_b, 0, 0)),
            pl.BlockSpec((1, D), lambda i: (0, 0)),
            pl.BlockSpec((D, p), lambda i: (0, 0)),
        ],
        out_specs=[pl.BlockSpec((tm, wd), lambda i: (i, 0)) for wd in widths],
        compiler_params=_cparams(("arbitrary",)),
        name="inproj",
    )(h, mod, g.reshape(1, D), w_bf16)


def _causal_conv(x_ref, cw_ref, xbuf, first):
    tt = x_ref.shape[0]

    @pl.when(first)
    def _():
        xbuf[0:SUBLANES, :] = jnp.zeros((SUBLANES, xbuf.shape[1]), F32)

    xbuf[SUBLANES:SUBLANES + tt, :] = x_ref[...]
    xe = xbuf[...]
    acc = cw_ref[0:1, :] * xe
    for w in range(1, CONV_W):
        acc = pltpu.roll(acc, 1, 0) + cw_ref[w:w + 1, :] * xe
    xbuf[0:SUBLANES, :] = xbuf[tt:tt + SUBLANES, :]
    return acc[SUBLANES:, :]


def _rglru_body(xa_ref, ga_ref, cw_ref, cb_ref, wg_ref, bg_ref, lam_ref, o_ref, xbuf, hc):
    t = pl.program_id(1)
    tt = xa_ref.shape[0]

    @pl.when(t == 0)
    def _():
        hc[...] = jnp.zeros_like(hc)

    xc = _causal_conv(xa_ref, cw_ref, xbuf, t == 0) + cb_ref[...]
    gates = _dot(xc, wg_ref[...]) + bg_ref[...]
    r = _sigmoid(gates[:, :W_A])
    i = _sigmoid(gates[:, W_A:])
    log_a = -RG_C * r * _softplus(-lam_ref[...])
    a = jnp.exp(log_a)
    b = jnp.sqrt(1.0 - a * a) * (i * xc)
    sub = lax.broadcasted_iota(I32, (tt, W_A), 0) % SUBLANES
    s = 1
    while s < SUBLANES:
        a_sh = pltpu.roll(a, s, 0)
        b_sh = pltpu.roll(b, s, 0)
        valid = sub >= s
        b = jnp.where(valid, a * b_sh + b, b)
        a = jnp.where(valid, a * a_sh, a)
        s *= 2
    carry = hc[0:1, :]
    groups = []
    for grp in range(tt // SUBLANES):
        rows = slice(grp * SUBLANES, (grp + 1) * SUBLANES)
        hg = b[rows] + a[rows] * carry
        carry = hg[SUBLANES - 1:SUBLANES, :]
        groups.append(hg)
    h = jnp.concatenate(groups, axis=0)
    hc[...] = jnp.broadcast_to(carry, hc.shape)
    o_ref[...] = jax.nn.gelu(ga_ref[...], approximate=True) * h


def _rglru(xa, ga, conv_w, conv_b, wg_bd, bg, lam, bsz, seq):
    tt = min(TT_SEQ, seq)
    per_b = seq // tt
    row = lambda b, t: (b * per_b + t, 0)
    const = lambda b, t: (0, 0)
    return pl.pallas_call(
        _rglru_body,
        out_shape=jax.ShapeDtypeStruct(xa.shape, F32),
        grid=(bsz, per_b),
        in_specs=[
            pl.BlockSpec((tt, W_A), row),
            pl.BlockSpec((tt, W_A), row),
            pl.BlockSpec((CONV_W, W_A), const),
            pl.BlockSpec((1, W_A), const),
            pl.BlockSpec((W_A, 2 * W_A), const),
            pl.BlockSpec((1, 2 * W_A), const),
            pl.BlockSpec((1, W_A), const),
        ],
        out_specs=pl.BlockSpec((tt, W_A), row),
        scratch_shapes=[pltpu.VMEM((tt + SUBLANES, W_A), F32), pltpu.VMEM((SUBLANES, W_A), F32)],
        compiler_params=_cparams(("arbitrary", "arbitrary")),
        name="rglru",
    )(xa, ga, conv_w, conv_b.reshape(1, W_A), wg_bd, bg.reshape(1, 2 * W_A), lam.reshape(1, W_A))


def _gla_body(q_ref, k_ref, v_ref, og_ref, gl_ref, wg_ref, bg_ref, gn_ref, o_ref, st_ref):
    t = pl.program_id(0)
    nbat, tt = q_ref.shape[0], q_ref.shape[1]

    @pl.when(t == 0)
    def _():
        st_ref[...] = jnp.zeros_like(st_ref)

    ri = lax.broadcasted_iota(I32, (CHUNK, CHUNK), 0)
    ci = lax.broadcasted_iota(I32, (CHUNK, CHUNK), 1)
    tri_incl = (ri >= ci).astype(F32)

    nseq = nbat * NH_B

    def chunk_group(cg, carry):
        blocks = [(b * SUB, (b + 1) * SUB) for b in range(CHUNK // SUB)]
        rows_of = []
        qh, kh, gh, egh, vh, ogh = [], [], [], [], [], []
        for sub in range(GLA_GROUP):
            r0 = pl.multiple_of((cg * GLA_GROUP + sub) * CHUNK, CHUNK)
            rows = pl.ds(r0, CHUNK)
            rows_of.append(rows)
            for b in range(nbat):
                x = _dot_hi(gl_ref[b, rows, :], wg_ref[...]) + bg_ref[...]
                lg = -_softplus(-x) * (1.0 / GATE_NORM)
                g = _dot_hi(tri_incl, lg)
                eg = jnp.exp(g)
                q_all = q_ref[b, rows, :] * (DK_B ** -0.5)
                k_all = k_ref[b, rows, :]
                v_all = v_ref[b, rows, :]
                og_all = og_ref[b, rows, :]
                for hh in range(NH_B):
                    ks_ = slice(hh * DKP_B, (hh + 1) * DKP_B)
                    vs_ = slice(hh * DV_B, (hh + 1) * DV_B)
                    qh.append(q_all[:, ks_])
                    kh.append(k_all[:, ks_])
                    gh.append(g[:, ks_])
                    egh.append(eg[:, ks_])
                    vh.append(v_all[:, vs_])
                    ogh.append(og_all[:, vs_])
        every = range(GLA_GROUP * nseq)
        g_last = [gh[h][CHUNK - 1:CHUNK, :] for h in every]
        att = [[None] * len(blocks) for _ in every]
        for h in every:
            for b, (lo, hi) in enumerate(blocks):
                ref_g = gh[h][lo:lo + 1, :]
                qb = qh[h][lo:hi] * jnp.exp(gh[h][lo:hi] - ref_g)
                kb = kh[h][:hi] * jnp.exp(ref_g - gh[h][:hi])
                causal = (lax.broadcasted_iota(I32, (SUB, hi), 1)
                          <= lax.broadcasted_iota(I32, (SUB, hi), 0) + lo)
                att[h][b] = jnp.where(causal, _dot_nt(qb, kb), 0.0)
        upd = [_dot_tn(vh[h], kh[h] * jnp.exp(g_last[h] - gh[h])) for h in every]
        o_intra = [[_dot(att[h][b], vh[h][:hi]) for b, (lo, hi) in enumerate(blocks)] for h in every]
        st = [st_ref[s] for s in range(nseq)]
        for sub in range(GLA_GROUP):
            heads = range(sub * nseq, (sub + 1) * nseq)
            o_inter = {h: _dot_nt(qh[h] * egh[h], st[h - sub * nseq]) for h in heads}
            outs = []
            for h in heads:
                st[h - sub * nseq] = st[h - sub * nseq] * jnp.exp(g_last[h]) + upd[h]
                o = o_inter[h] + jnp.concatenate(o_intra[h], axis=0)
                o = o * lax.rsqrt(jnp.mean(o * o, axis=-1, keepdims=True) + EPS) * gn_ref[...]
                outs.append(o * _silu(ogh[h]))
            for b in range(nbat):
                o_ref[b, rows_of[sub], :] = jnp.concatenate(outs[b * NH_B:(b + 1) * NH_B], axis=1)
        for s in range(nseq):
            st_ref[s] = st[s]
        return carry

    lax.fori_loop(0, tt // (CHUNK * GLA_GROUP), chunk_group, 0)


def _gla(q, k, v, og, gl, wg2p, bg2p, gnorm, bsz, seq):
    tt = min(TT_SEQ, seq)
    row = lambda t: (0, t, 0)
    const = lambda t: (0, 0)
    r3 = lambda a: a.reshape(bsz, seq, a.shape[-1])
    out = pl.pallas_call(
        _gla_body,
        out_shape=jax.ShapeDtypeStruct((bsz, seq, V_B), F32),
        grid=(seq // tt,),
        in_specs=[
            pl.BlockSpec((bsz, tt, KP_B), row),
            pl.BlockSpec((bsz, tt, KP_B), row),
            pl.BlockSpec((bsz, tt, V_B), row),
            pl.BlockSpec((bsz, tt, V_B), row),
            pl.BlockSpec((bsz, tt, LANES), row),
            pl.BlockSpec((LANES, KP_B), const),
            pl.BlockSpec((1, KP_B), const),
            pl.BlockSpec((1, DV_B), const),
        ],
        out_specs=pl.BlockSpec((bsz, tt, V_B), row),
        scratch_shapes=[pltpu.VMEM((bsz * NH_B, DV_B, DKP_B), F32)],
        compiler_params=_cparams(("arbitrary",)),
        name="gla",
    )(r3(q), r3(k), r3(v), r3(og), r3(gl), wg2p, bg2p, gnorm.reshape(1, DV_B))
    return out.reshape(bsz * seq, V_B)


def _unit_lower_solve(lows, rhss, ri, ci):
    n = range(len(lows))
    same = (ri // SUB) == (ci // SUB)
    eye = (ri == ci).astype(F32)
    dg = [jnp.where(same, lows[h], 0.0) for h in n]
    off = [lows[h] - dg[h] for h in n]
    tinv = [eye - dg[h] for h in n]
    pw = dg
    e = 2
    while e < SUB:
        pw = [_dot(pw[h], pw[h]) for h in n]
        tinv = [tinv[h] + _dot(tinv[h], pw[h]) for h in n]
        e *= 2
    nmat = [_dot(tinv[h], off[h]) for h in n]
    x = [_dot(tinv[h], rhss[h]) for h in n]
    sol = [[x[h][0:SUB]] for h in n]
    for blk in range(1, CHUNK // SUB):
        lo, hi = blk * SUB, (blk + 1) * SUB
        for h in n:
            prev = jnp.concatenate(sol[h], axis=0)
            sol[h].append(x[h][lo:hi] - _dot(nmat[h][lo:hi, :lo], prev))
    return [jnp.concatenate(sol[h], axis=0) for h in n]


def _delta_body(q_ref, k_ref, v_ref, z_ref, ba_ref, cq_ref, ck_ref, cv_ref, alog_ref, dtb_ref, dn_ref,
                o_ref, qbuf, kbuf, vbuf, qs, ks, vs, st_ref):
    t = pl.program_id(0)
    nbat, tt = q_ref.shape[0], q_ref.shape[1]

    @pl.when(t == 0)
    def _():
        st_ref[...] = jnp.zeros_like(st_ref)

    for b in range(nbat):
        qs[b] = _silu(_causal_conv(q_ref.at[b], cq_ref, qbuf.at[b], t == 0))
        ks[b] = _silu(_causal_conv(k_ref.at[b], ck_ref, kbuf.at[b], t == 0))
        vs[b] = _silu(_causal_conv(v_ref.at[b], cv_ref, vbuf.at[b], t == 0))

    ri = lax.broadcasted_iota(I32, (CHUNK, CHUNK), 0)
    ci = lax.broadcasted_iota(I32, (CHUNK, CHUNK), 1)
    tri_incl = (ri >= ci).astype(F32)
    incl = ri >= ci
    strict = ri > ci

    nseq = nbat * NH_C

    def chunk_group(cg, carry):
        rows_of, seqs = [], []
        qh, kh, vh, zh, beta, gcol, egcol, g_last, decay = [], [], [], [], [], [], [], [], []
        for sub in range(DELTA_GROUP):
            r0 = pl.multiple_of((cg * DELTA_GROUP + sub) * CHUNK, CHUNK)
            rows = pl.ds(r0, CHUNK)
            rows_of.append(rows)
            for b in range(nbat):
                ba = ba_ref[b, rows, :]
                beta_all = _sigmoid(ba)
                g_all = -jnp.exp(alog_ref[...]) * _softplus(ba + dtb_ref[...])
                gcum = _dot_hi(tri_incl, g_all)
                gcum_t = gcum.T
                egc = jnp.exp(gcum)
                q_all, k_all, v_all, z_all = qs[b, rows, :], ks[b, rows, :], vs[b, rows, :], z_ref[b, rows, :]
                for hh in range(NH_C):
                    hs = slice(hh * DH_C, (hh + 1) * DH_C)
                    q_, k_ = q_all[:, hs], k_all[:, hs]
                    qh.append(q_ * lax.rsqrt(jnp.sum(q_ * q_, axis=-1, keepdims=True) + EPS) * (DH_C ** -0.5))
                    kh.append(k_ * lax.rsqrt(jnp.sum(k_ * k_, axis=-1, keepdims=True) + EPS))
                    vh.append(v_all[:, hs])
                    zh.append(z_all[:, hs])
                    beta.append(beta_all[:, hh:hh + 1])
                    gcol.append(gcum[:, NH_C + hh:NH_C + hh + 1])
                    egcol.append(egc[:, NH_C + hh:NH_C + hh + 1])
                    g_last.append(gcum[CHUNK - 1:CHUNK, NH_C + hh:NH_C + hh + 1])
                    grow = gcum_t[NH_C + hh:NH_C + hh + 1, :]
                    decay.append(jnp.exp(jnp.where(incl, gcol[-1] - grow, -jnp.inf)))
        every = range(DELTA_GROUP * nseq)
        kbeta = [kh[h] * beta[h] for h in every]
        kq = [_dot_nt(jnp.concatenate([kbeta[h], qh[h]], axis=0), kh[h]) for h in every]
        low = [jnp.where(strict, kq[h][:CHUNK] * decay[h], 0.0) for h in every]
        rhs = [jnp.concatenate([vh[h] * beta[h], kbeta[h] * egcol[h]], axis=1) for h in every]
        sol = _unit_lower_solve(low, rhs, ri, ci)
        st = [st_ref[s] for s in range(nseq)]
        for sub in range(DELTA_GROUP):
            heads = range(sub * nseq, (sub + 1) * nseq)
            ws = {h: _dot(jnp.concatenate([sol[h][:, DH_C:], qh[h] * egcol[h]], axis=0), st[h - sub * nseq])
                  for h in heads}
            v_new = {h: sol[h][:, :DH_C] - ws[h][:CHUNK] for h in heads}
            o_intra = {h: _dot(kq[h][CHUNK:] * decay[h], v_new[h]) for h in heads}
            upd = {h: _dot_tn(kh[h] * jnp.exp(g_last[h] - gcol[h]), v_new[h]) for h in heads}
            outs = []
            for h in heads:
                st[h - sub * nseq] = st[h - sub * nseq] * jnp.exp(g_last[h]) + upd[h]
                o = ws[h][CHUNK:] + o_intra[h]
                o = o * lax.rsqrt(jnp.mean(o * o, axis=-1, keepdims=True) + EPS) * dn_ref[...]
                outs.append(o * _silu(zh[h]))
            for b in range(nbat):
                o_ref[b, rows_of[sub], :] = jnp.concatenate(outs[b * NH_C:(b + 1) * NH_C], axis=1)
        for s in range(nseq):
            st_ref[s] = st[s]
        return carry

    lax.fori_loop(0, tt // (CHUNK * DELTA_GROUP), chunk_group, 0)


def _delta(q, k, v, z, ba, cq, ck, cv, alog, dtb, dnorm, bsz, seq):
    tt = min(TT_SEQ, seq)
    row = lambda t: (0, t, 0)
    const = lambda t: (0, 0)
    big = pl.BlockSpec((bsz, tt, W_C), row)
    cw = pl.BlockSpec((CONV_W, W_C), const)
    vec = pl.BlockSpec((1, LANES), const)
    r3 = lambda a: a.reshape(bsz, seq, a.shape[-1])
    out = pl.pallas_call(
        _delta_body,
        out_shape=jax.ShapeDtypeStruct((bsz, seq, W_C), F32),
        grid=(seq // tt,),
        in_specs=[big, big, big, big, pl.BlockSpec((bsz, tt, LANES), row), cw, cw, cw, vec, vec, vec],
        out_specs=big,
        scratch_shapes=[pltpu.VMEM((bsz, tt + SUBLANES, W_C), F32)] * 3 + [pltpu.VMEM((bsz, tt, W_C), F32)] * 3
        + [pltpu.VMEM((bsz * NH_C, DH_C, DH_C), F32)],
        compiler_params=_cparams(("arbitrary",)),
        name="delta",
    )(r3(q), r3(k), r3(v), r3(z), r3(ba), cq, ck, cv, alog, dtb, dnorm)
    return out.reshape(bsz * seq, W_C)


def _outproj_router_body(n_parts, *refs):
    y_refs = refs[:n_parts]
    w_refs = refs[n_parts:2 * n_parts]
    (h_ref, mod_ref, g2_ref, wr_hi_ref, wr_lo_ref, br_ref,
     h1_ref, u2_ref, info_ref, cnt_ref, carry, uhi_s, ulo_s) = refs[2 * n_parts:]
    i = pl.program_id(0)
    tm = h_ref.shape[0]

    @pl.when(i == 0)
    def _():
        carry[...] = jnp.zeros_like(carry)
        uhi_s[...] = jnp.zeros_like(uhi_s)
        ulo_s[...] = jnp.zeros_like(ulo_s)

    u_hi = uhi_s[...]
    u_lo = ulo_s[...]
    w_hi = wr_hi_ref[...]
    logits = (jnp.dot(u_hi, w_hi, preferred_element_type=F32)
              + jnp.dot(u_hi, wr_lo_ref[...], preferred_element_type=F32)
              + jnp.dot(u_lo, w_hi, preferred_element_type=F32)) + br_ref[...]
    m = None
    for y_ref, w_ref in zip(y_refs, w_refs):
        part = _dot(y_ref[...], w_ref[...])
        m = part if m is None else m + part
    h1 = h_ref[...] + mod_ref[2:3, :] * m
    h1_ref[...] = h1
    u2 = _rms_mod(h1, g2_ref[...], mod_ref[4:5, :], mod_ref[3:4, :])
    _to_row_tiles(u2_ref, u2)
    nxt_hi = u2.astype(BF16)
    uhi_s[...] = nxt_hi
    ulo_s[...] = (u2 - nxt_hi.astype(F32)).astype(BF16)

    live = (i > 0).astype(F32)
    lane = lax.broadcasted_iota(I32, (tm, LANES), 1)
    neg = -jnp.inf
    count = carry[0:1, :]
    lgp = jnp.where((lane >= N_EXP) & (lane < N_EXP + N_GROUPS), logits, neg)
    gmax = jnp.max(lgp, axis=-1, keepdims=True)
    pg_top = 1.0 / jnp.sum(jnp.exp(lgp - gmax), axis=-1, keepdims=True)
    g_idx = jnp.min(jnp.where(lgp == gmax, lane, LANES), axis=-1, keepdims=True) - N_EXP
    in_group = (lane // EPG == g_idx) & (lane < N_EXP)
    le = jnp.where(in_group, logits, neg)
    emax = jnp.max(le, axis=-1, keepdims=True)
    pe = jnp.exp(le - emax)
    pe = jnp.where(in_group, pe / jnp.sum(pe, axis=-1, keepdims=True), -1.0)
    p0 = jnp.max(pe, axis=-1, keepdims=True)
    e0 = jnp.min(jnp.where(pe == p0, lane, LANES), axis=-1, keepdims=True)
    pe1 = jnp.where(lane == e0, -1.0, pe)
    p1 = jnp.max(pe1, axis=-1, keepdims=True)
    e1 = jnp.min(jnp.where(pe1 == p1, lane, LANES), axis=-1, keepdims=True)
    den = p0 + p1
    w0 = pg_top * p0 / den
    w1 = pg_top * p1 / den
    hot = ((lane == e0) | (lane == e1)).astype(F32) * live
    ri = lax.broadcasted_iota(I32, (tm, tm), 0)
    ci = lax.broadcasted_iota(I32, (tm, tm), 1)
    before = _dot((ri > ci).astype(F32), hot) + count
    r0 = jnp.sum(jnp.where(lane == e0, before, 0.0), axis=-1, keepdims=True)
    r1 = jnp.sum(jnp.where(lane == e1, before, 0.0), axis=-1, keepdims=True)
    info = jnp.where(lane == 0, e0.astype(F32), 0.0)
    info = jnp.where(lane == 1, e1.astype(F32), info)
    info = jnp.where(lane == 2, r0, info)
    info = jnp.where(lane == 3, r1, info)
    info = jnp.where(lane == 4, w0, info)
    info = jnp.where(lane == 5, w1, info)
    info_ref[...] = info
    carry[...] = jnp.broadcast_to(count + jnp.sum(hot, axis=0, keepdims=True), carry.shape)
    cnt_ref[...] = carry[...]


def _outproj_router(ys, ws, h, mod, g2, wr_hi, wr_lo, br, seq):
    t_rows = h.shape[0]
    tm = min(TM_PROJ, seq)
    per_b = seq // tm
    n = len(ys)
    nt = t_rows // tm
    row = lambda i: (jnp.minimum(i, nt - 1), 0)
    prev = lambda i: (jnp.maximum(i - 1, 0), 0)
    const = lambda i: (0, 0)
    in_specs = [pl.BlockSpec((tm, y.shape[1]), row) for y in ys]
    in_specs += [pl.BlockSpec(w.shape, const) for w in ws]
    in_specs += [
        pl.BlockSpec((tm, D), row),
        pl.BlockSpec((None, 6, D), lambda i: (jnp.minimum(i, nt - 1) // per_b, 0, 0)),
        pl.BlockSpec((1, D), const),
        pl.BlockSpec((D, LANES), const),
        pl.BlockSpec((D, LANES), const),
        pl.BlockSpec((1, LANES), const),
    ]
    return pl.pallas_call(
        functools.partial(_outproj_router_body, n),
        out_shape=[
            jax.ShapeDtypeStruct((t_rows, D), F32),
            jax.ShapeDtypeStruct((t_rows * RT, LANES), F32),
            jax.ShapeDtypeStruct((t_rows, LANES), F32),
            jax.ShapeDtypeStruct((SUBLANES, LANES), F32),
        ],
        grid=(nt + 1,),
        in_specs=in_specs,
        out_specs=[
            pl.BlockSpec((tm, D), row),
            pl.BlockSpec((tm * RT, LANES), row),
            pl.BlockSpec((tm, LANES), prev),
            pl.BlockSpec((SUBLANES, LANES), const),
        ],
        scratch_shapes=[pltpu.VMEM((SUBLANES, LANES), F32), pltpu.VMEM((tm, D), BF16), pltpu.VMEM((tm, D), BF16)],
        compiler_params=_cparams(("arbitrary",)),
        name="outproj_router",
    )(*ys, *ws, h, mod, g2.reshape(1, D), wr_hi, wr_lo, br)


RT = D // LANES
assert RT == SUBLANES


def _to_row_tiles(ref, x, row0=0):
    n = x.shape[0]
    for c in range(RT):
        ref[pl.ds(row0 * RT + c, n, stride=RT), :] = x[:, c * LANES:(c + 1) * LANES]


def _from_row_tiles(ref, n):
    return jnp.concatenate([ref[pl.ds(c, n, stride=RT), :] for c in range(RT)], axis=1)


def _row_copy(src, src_row, dst, dst_row, sem):
    return pltpu.make_async_copy(src.at[pl.ds(pl.multiple_of(src_row * RT, RT), RT)],
                                 dst.at[pl.ds(pl.multiple_of(dst_row * RT, RT), RT)], sem)


def _pos_body(info_ref, sp_ref, o_ref):
    info = info_ref[...]
    td = info.shape[0]
    lane = lax.broadcasted_iota(I32, (td, LANES), 1)
    sp = sp_ref[...]
    packed = jnp.zeros((td, LANES), F32)
    for k in range(TOP_K):
        e = info[:, k:k + 1].astype(I32)
        start = jnp.sum(jnp.where(lane == e, sp, 0.0), axis=-1, keepdims=True)
        packed = jnp.where(lane == k, start + info[:, TOP_K + k:TOP_K + k + 1], packed)
    pick = (lax.broadcasted_iota(I32, (SUBLANES, LANES), 0)
            == lax.broadcasted_iota(I32, (SUBLANES, LANES), 1)).astype(F32)
    o_ref[...] = lax.dot_general(pick, packed, (((1,), (1,)), ((), ())), precision=HI,
                                 preferred_element_type=F32).astype(I32)


def _moe_pos(info, sp_row):
    t_rows = info.shape[0]
    tp = min(TP_MOE, t_rows)
    return pl.pallas_call(
        _pos_body,
        out_shape=jax.ShapeDtypeStruct((SUBLANES, t_rows), I32),
        grid=(t_rows // tp,),
        in_specs=[pl.BlockSpec((tp, LANES), lambda i: (i, 0)), pl.BlockSpec((1, LANES), lambda i: (0, 0))],
        out_specs=pl.BlockSpec((SUBLANES, tp), lambda i: (0, i)),
        compiler_params=_cparams(("arbitrary",)),
        name="moe_pos",
    )(info, sp_row)


def _dispatch_body(be_ref, nu_ref, pos_ref, u_ref, xb_out, zero_s, sems):
    i = pl.program_id(0)
    td = u_ref.shape[0] // RT
    nb = be_ref.shape[0]

    @pl.when(i == 0)
    def _():
        zero_s[...] = jnp.zeros_like(zero_s)

        def needs_zero(b):
            return (b >= nu_ref[0] - 1) | (be_ref[jnp.minimum(b + 1, nb - 1)] != be_ref[b])

        def block_copy(b):
            return pltpu.make_async_copy(zero_s, xb_out.at[pl.ds(b * (BM_MOE * RT), BM_MOE * RT)], sems.at[1])

        def zissue(b, c):
            @pl.when(needs_zero(b))
            def _():
                block_copy(b).start()
            return c

        def zdrain(b, c):
            @pl.when(needs_zero(b))
            def _():
                block_copy(b).wait()
            return c

        lax.fori_loop(0, nb, zissue, 0)
        lax.fori_loop(0, nb, zdrain, 0)

    def issue(j, c):
        for k in range(TOP_K):
            _row_copy(u_ref, j, xb_out, pos_ref[k, j], sems.at[0]).start(priority=k)
        return c

    lax.fori_loop(0, td, issue, 0, unroll=DMA_UNROLL)

    def drain(j, c):
        for k in range(TOP_K):
            _row_copy(u_ref, j, xb_out, pos_ref[k, j], sems.at[0]).wait()
        return c

    lax.fori_loop(0, td, drain, 0, unroll=DMA_UNROLL)


def _dispatch(blk_e, n_used, pos, u2, n_rows, td):
    t_rows = u2.shape[0] // RT
    nstep = t_rows // td
    return pl.pallas_call(
        _dispatch_body,
        out_shape=jax.ShapeDtypeStruct((n_rows * RT, LANES), F32),
        grid_spec=pltpu.PrefetchScalarGridSpec(
            num_scalar_prefetch=2,
            grid=(nstep,),
            in_specs=[
                pl.BlockSpec((SUBLANES, td), lambda i, be, nu: (0, i), memory_space=pltpu.SMEM),
                pl.BlockSpec((td * RT, LANES), lambda i, be, nu: (i, 0)),
            ],
            out_specs=pl.BlockSpec(memory_space=pl.ANY),
            scratch_shapes=[pltpu.VMEM((BM_MOE * RT, LANES), F32), pltpu.SemaphoreType.DMA((2,))],
        ),
        compiler_params=_cparams(("arbitrary",)),
        name="moe_dispatch",
    )(blk_e, n_used, pos, u2)


def _ffn_body(be_ref, nu_ref, x_ref, w1_ref, w3_ref, w2_ref, o_ref, w13_s, w2_s):
    i = pl.program_id(0)
    prev = be_ref[jnp.maximum(i - 1, 0)]
    fresh = (i == 0) | (be_ref[i] != prev)

    @pl.when(fresh & (i < nu_ref[0]))
    def _():
        w13_s[:, :D_EXP] = w1_ref[...].astype(BF16)
        w13_s[:, D_EXP:] = w3_ref[...].astype(BF16)
        w2_s[...] = w2_ref[...].astype(BF16)

    @pl.when(i < nu_ref[0])
    def _():
        x = _from_row_tiles(x_ref, BM_MOE).astype(BF16)
        a = jnp.dot(x, w13_s[...], preferred_element_type=F32)
        hid = _silu(a[:, :D_EXP]) * a[:, D_EXP:]
        _to_row_tiles(o_ref, jnp.dot(hid.astype(BF16), w2_s[...], preferred_element_type=F32))

    @pl.when(i >= nu_ref[0])
    def _():
        o_ref[...] = jnp.zeros_like(o_ref)


def _ffn(blk_e, n_used, xb, w1, w3, w2, layer):
    n_rows = xb.shape[0] // RT
    nb = n_rows // BM_MOE

    def last_used(i, nu):
        return jnp.maximum(jnp.minimum(i, nu[0] - 1), 0)

    def xmap(i, be, nu):
        return (last_used(i, nu), 0)

    def wmap(i, be, nu):
        return (layer, be[last_used(i, nu)], 0, 0)

    return pl.pallas_call(
        _ffn_body,
        out_shape=jax.ShapeDtypeStruct((n_rows * RT, LANES), F32),
        grid_spec=pltpu.PrefetchScalarGridSpec(
            num_scalar_prefetch=2,
            grid=(nb,),
            in_specs=[
                pl.BlockSpec((BM_MOE * RT, LANES), xmap),
                pl.BlockSpec((None, None, D, D_EXP), wmap),
                pl.BlockSpec((None, None, D, D_EXP), wmap),
                pl.BlockSpec((None, None, D_EXP, D), wmap),
            ],
            out_specs=pl.BlockSpec((BM_MOE * RT, LANES), lambda i, be, nu: (i, 0)),
            scratch_shapes=[pltpu.VMEM((D, 2 * D_EXP), BF16), pltpu.VMEM((D_EXP, D), BF16)],
        ),
        compiler_params=_cparams(("arbitrary",)),
        name="moe_ffn",
    )(blk_e, n_used, xb, w1, w3, w2)


def _combine_body(final, pos_ref, nxt_ref, yb_hbm, h_ref, info_ref, mod_ref, fn_ref, o_ref, ybuf, sems):
    i = pl.program_id(0)
    n = pl.num_programs(0)
    td = h_ref.shape[0]
    slot = i % 2

    def gather(p_ref, s, wait):
        def body(j, c):
            for k in range(TOP_K):
                cp = _row_copy(yb_hbm, p_ref[k, j], ybuf.at[s, k], j, sems.at[s])
                cp.wait() if wait else cp.start(priority=k)
            return c

        lax.fori_loop(0, td, body, 0, unroll=DMA_UNROLL)

    @pl.when(i == 0)
    def _():
        gather(pos_ref, slot, False)

    @pl.when(i + 1 < n)
    def _():
        gather(nxt_ref, 1 - slot, False)

    gather(pos_ref, slot, True)
    info = info_ref[...]
    y = (info[:, 4:5] * _from_row_tiles(ybuf.at[slot, 0], td)
         + info[:, 5:6] * _from_row_tiles(ybuf.at[slot, 1], td))
    h2 = h_ref[...] + mod_ref[5:6, :] * y
    if final:
        h2 = h2 * lax.rsqrt(jnp.mean(h2 * h2, axis=-1, keepdims=True) + EPS) * fn_ref[...]
    o_ref[...] = h2


def _combine(pos, yb, h1, info, mod, fnorm, seq, td, final):
    t_rows = h1.shape[0]
    per_b = seq // td
    nstep = t_rows // td
    row = lambda i: (i, 0)
    return pl.pallas_call(
        functools.partial(_combine_body, final),
        out_shape=jax.ShapeDtypeStruct((t_rows, D), F32),
        grid=(nstep,),
        in_specs=[
            pl.BlockSpec((SUBLANES, td), lambda i: (0, i), memory_space=pltpu.SMEM),
            pl.BlockSpec((SUBLANES, td), lambda i: (0, jnp.minimum(i + 1, nstep - 1)),
                         memory_space=pltpu.SMEM),
            pl.BlockSpec(memory_space=pl.ANY),
            pl.BlockSpec((td, D), row),
            pl.BlockSpec((td, LANES), row),
            pl.BlockSpec((None, 6, D), lambda i: (i // per_b, 0, 0)),
            pl.BlockSpec((1, D), lambda i: (0, 0)),
        ],
        out_specs=pl.BlockSpec((td, D), row),
        scratch_shapes=[pltpu.VMEM((2, TOP_K, td * RT, LANES), F32), pltpu.SemaphoreType.DMA((2,))],
        compiler_params=_cparams(("arbitrary",)),
        name="moe_combine",
    )(pos, pos, yb, h1, info, mod, fnorm.reshape(1, D))


def _moe(u2, info, counts, h1, mod, w1, w3, w2, layer, fnorm, seq, final):
    t_rows = h1.shape[0]
    td = min(TD_MOE, seq)
    nb = (t_rows * TOP_K) // BM_MOE + N_EXP
    n_rows = nb * BM_MOE
    cnt = counts[0, :N_EXP].astype(I32)
    padded = (cnt + BM_MOE - 1) // BM_MOE * BM_MOE
    end_p = jnp.cumsum(padded)
    start_p = end_p - padded
    sp_row = jnp.zeros((1, LANES), F32).at[0, :N_EXP].set(start_p.astype(F32))
    blk_e = jnp.minimum(jnp.sum(jnp.arange(nb, dtype=I32)[:, None] * BM_MOE >= end_p[None, :], axis=1),
                        N_EXP - 1).astype(I32)
    n_used = (end_p[-1:] // BM_MOE).astype(I32)
    pos = _moe_pos(info, sp_row)
    xb = _dispatch(blk_e, n_used, pos, u2, n_rows, td)
    yb = _ffn(blk_e, n_used, xb, w1, w3, w2, layer)
    return _combine(pos, yb, h1, info, mod, fnorm, seq, td, final)


def _pad_cols(w, n):
    return jnp.pad(w, ((0, 0), (0, n - w.shape[1])))


def _pad_heads(w):
    w = w.reshape(w.shape[:-1] + (NH_B, DK_B))
    w = jnp.pad(w, [(0, 0)] * (w.ndim - 1) + [(0, DKP_B - DK_B)])
    return w.reshape(w.shape[:-2] + (KP_B,))


def _block_diag(w):
    nh, hw, _ = w.shape
    eye = jnp.eye(nh, dtype=w.dtype)
    return (eye[:, None, :, None] * w[:, :, None, :]).reshape(nh * hw, nh * hw)


def _router_weights(w_grp, b_grp, w_rt, b_rt):
    w = _pad_cols(jnp.concatenate([w_rt, w_grp], axis=1), LANES)
    b = _pad_cols(jnp.concatenate([b_rt, b_grp]).reshape(1, -1), LANES)
    w_hi = w.astype(BF16)
    w_lo = (w - w_hi.astype(F32)).astype(BF16)
    return w_hi, w_lo, b


def kernel(x, c, norm1, norm2, w_ada, b_ada, w_in_ab, conv_a_w, conv_a_b, rg_wa, rg_ba, rg_wx, rg_bx, rg_lam, gla_wg2, gla_bg2, gla_norm, w_out_ab, w_in_c, conv_c_w, dn_a_log, dn_dt_bias, dn_norm, w_out_c, moe_w_grp, moe_b_grp, moe_w_rt, moe_b_rt, moe_w1, moe_w3, moe_w2, final_norm):
    bsz, seq, _ = x.shape
    depth = w_ada.shape[0]
    t_rows = bsz * seq
    mod_all = _ada_mod(c, w_ada, b_ada)
    h = x.reshape(t_rows, D)
    for layer in range(depth):
        mod = mod_all[layer]
        j = layer // 2
        if layer % 2 == 0:
            wi = w_in_ab[j]
            o_q, o_k, o_v, o_gl = 2 * W_A, 2 * W_A + K_B, 2 * W_A + 2 * K_B, 2 * W_A + 2 * K_B + 2 * V_B
            w_in = jnp.concatenate([
                wi[:, :o_q], _pad_heads(wi[:, o_q:o_k]), _pad_heads(wi[:, o_k:o_v]), wi[:, o_v:o_gl],
                _pad_cols(wi[:, o_gl:], LANES)], axis=1).astype(BF16)
            xa, ga, q, k, v, og, gl = _inproj(h, mod, norm1[layer], w_in,
                                              (W_A, W_A, KP_B, KP_B, V_B, V_B, LANES), seq)
            wg_bd = jnp.concatenate([_block_diag(rg_wa[j]), _block_diag(rg_wx[j])], axis=1).astype(BF16)
            bg = jnp.concatenate([rg_ba[j], rg_bx[j]])
            ya = _rglru(xa, ga, conv_a_w[j], conv_a_b[j], wg_bd, bg, rg_lam[j], bsz, seq)
            wg2p = jnp.pad(_pad_heads(gla_wg2[j]), ((0, LANES - R_GATE), (0, 0)))
            bg2p = _pad_heads(gla_bg2[j].reshape(1, K_B))
            ob = _gla(q, k, v, og, gl, wg2p, bg2p, gla_norm[j], bsz, seq)
            ys = (ya, ob)
            ws = (w_out_ab[j][:W_A].astype(BF16), w_out_ab[j][W_A:].astype(BF16))
        else:
            p_c = w_in_c.shape[2]
            w_in = _pad_cols(w_in_c[j], p_c - 2 * NH_C + LANES).astype(BF16)
            q, k, v, z, ba = _inproj(h, mod, norm1[layer], w_in, (W_C, W_C, W_C, W_C, LANES), seq)
            cw = conv_c_w[j]
            lanes_c = jnp.zeros((1, LANES), F32)
            alog = lanes_c.at[0, NH_C:2 * NH_C].set(dn_a_log[j])
            dtb = lanes_c.at[0, NH_C:2 * NH_C].set(dn_dt_bias[j])
            o = _delta(q, k, v, z, ba, cw[:, :W_C], cw[:, W_C:2 * W_C], cw[:, 2 * W_C:], alog, dtb,
                       dn_norm[j].reshape(1, DH_C), bsz, seq)
            ys = (o,)
            ws = (w_out_c[j].astype(BF16),)
        rw = _router_weights(moe_w_grp[layer], moe_b_grp[layer], moe_w_rt[layer], moe_b_rt[layer])
        h1, u2, info, counts = _outproj_router(ys, ws, h, mod, norm2[layer], *rw, seq)
        h = _moe(u2, info, counts, h1, mod, moe_w1, moe_w3, moe_w2, layer, final_norm, seq,
                 final=layer == depth - 1)
    return h.reshape(bsz, seq, D)
```
